```python
import math
import jax
import jax.numpy as jnp
from jax import lax
import numpy as np

D_MODEL = 1024
BATCH = 8
SEQ = 2048
DEPTH = 4

CTX_LEN = 256
GRID_W = 64
N_BRANCH = 3
N_MOD = 9
D_FF = 2816
MLA_HEADS = 8
MLA_NOPE = 64
MLA_ROPE = 32
MLA_QK = MLA_NOPE + MLA_ROPE
MLA_V = 64
MLA_Q_RANK = 384
MLA_KV_RANK = 256
DIFF_HEADS = 4
DIFF_DIM = 64
DIFF_V = 2 * DIFF_DIM
CONV_WIDTH = 512
CONV_K = 3
BRANCH_WIDTH = 512
Q_BLOCK = 128
ROPE_THETA = 10000.0
EPS = 1e-6

IN_SIZES = (MLA_Q_RANK, MLA_KV_RANK, MLA_ROPE,
            DIFF_HEADS * 2 * DIFF_DIM, DIFF_HEADS * 2 * DIFF_DIM, DIFF_HEADS * DIFF_V,
            CONV_WIDTH, CONV_WIDTH, CONV_WIDTH,
            N_BRANCH * D_MODEL)
IN_WIDTH = sum(IN_SIZES)
IN_OFFSETS = tuple(sum(IN_SIZES[:i + 1]) for i in range(len(IN_SIZES) - 1))

kernel_name = 'hybrid_mla_diffattn_shortconv_macaron_dit'


def rms_norm(x, g):
    xf = x.astype(jnp.float32)
    y = xf * lax.rsqrt(jnp.mean(xf * xf, axis=-1, keepdims=True) + EPS)
    return (y * g.astype(jnp.float32)).astype(x.dtype)


def modulate(x, g, shift, scale):
    return rms_norm(x, g) * (1.0 + scale) + shift


def swiglu(h, w_up, w_down):
    a, b = jnp.split(h @ w_up, 2, axis=-1)
    return (jax.nn.silu(a) * b) @ w_down


def axial_angles(n_tokens, rot_dim):
    rows = n_tokens // GRID_W
    row = jnp.broadcast_to(jnp.arange(rows, dtype=jnp.float32)[:, None], (rows, GRID_W)).reshape(-1)
    col = jnp.broadcast_to(jnp.arange(GRID_W, dtype=jnp.float32)[None, :], (rows, GRID_W)).reshape(-1)
    quarter = rot_dim // 4
    inv_freq = ROPE_THETA ** (-jnp.arange(quarter, dtype=jnp.float32) / quarter)
    return jnp.concatenate([row[:, None] * inv_freq, col[:, None] * inv_freq], axis=-1)


def apply_rope2d(x, ang):
    r = x.shape[-1]
    q4 = r // 4
    xs = x.astype(jnp.float32).reshape(x.shape[:-1] + (2, 2, q4))
    extra = x.ndim - 3
    a = ang.reshape((ang.shape[0],) + (1,) * extra + (2, q4))
    cos, sin = jnp.cos(a), jnp.sin(a)
    x1, x2 = xs[..., 0, :], xs[..., 1, :]
    out = jnp.stack([x1 * cos - x2 * sin, x2 * cos + x1 * sin], axis=-2)
    return out.reshape(x.shape).astype(x.dtype)


def rope_tail(x, ang, rot_dim):
    return jnp.concatenate([x[..., :-rot_dim], apply_rope2d(x[..., -rot_dim:], ang)], axis=-1)


def attend(q, k, v, mix, scale):
    b, sq, h, m, dk = q.shape
    blk = min(Q_BLOCK, sq)
    nb = sq // blk
    qb = jnp.moveaxis(q.reshape(b, nb, blk, h, m, dk), 1, 0)

    def one_block(qi):
        s = jnp.einsum('bqhmd,bkhmd->bhmqk', qi, k, preferred_element_type=jnp.float32) * scale
        p = jnp.einsum('bhmqk,m->bhqk', jax.nn.softmax(s, axis=-1), mix)
        return jnp.einsum('bhqk,bkhe->bqhe', p.astype(v.dtype), v)

    o = lax.map(one_block, qb)
    return jnp.moveaxis(o, 0, 1).reshape(b, sq, h, v.shape[-1])


def short_conv(u, w):
    t = u.shape[1]
    half = CONV_K // 2
    up = jnp.pad(u, ((0, 0), (half, half), (0, 0)))
    y = w[0] * up[:, :t]
    for j in range(1, CONV_K):
        y = y + w[j] * up[:, j:j + t]
    return y


def merge(branches, gate_logits, w_br, w_o):
    b, s, _ = gate_logits.shape
    y = jnp.stack(branches, axis=2)
    g = jax.nn.sigmoid(gate_logits.reshape(b, s, N_BRANCH, D_MODEL))
    m = jnp.sum(g * jnp.einsum('bsnw,nwd->bsnd', y, w_br), axis=2)
    return m @ w_o


def mixer(h_ctx, h_lat, ang_mla, ang_diff, p, lam_init, need_ctx):
    b, s, _ = h_lat.shape
    n_ctx = h_ctx.shape[1]
    t = n_ctx + s
    h_all = jnp.concatenate([h_ctx, h_lat], axis=1)
    proj = h_all @ p['w_in']
    (c_q, c_kv, k_rope, dq, dk, dv, conv_b, conv_c, conv_x, gate_logits) = jnp.split(proj, IN_OFFSETS, axis=-1)

    q = (rms_norm(c_q, p['g_cq']) @ p['w_uq']).reshape(b, t, MLA_HEADS, MLA_QK)
    kv = (rms_norm(c_kv, p['g_ckv']) @ p['w_ukv']).reshape(b, t, MLA_HEADS, MLA_NOPE + MLA_V)
    k_nope, v = kv[..., :MLA_NOPE], kv[..., MLA_NOPE:]
    k = jnp.concatenate([k_nope, jnp.broadcast_to(k_rope[:, :, None, :], (b, t, MLA_HEADS, MLA_ROPE))], axis=-1)
    q = rms_norm(q, p['g_q_mla'])
    k = rms_norm(k, p['g_k_mla'])
    k = jnp.concatenate([k[:, :n_ctx], rope_tail(k[:, n_ctx:], ang_mla, MLA_ROPE)], axis=1)
    q_lat = rope_tail(q[:, n_ctx:], ang_mla, MLA_ROPE)
    one = jnp.ones((1,), jnp.float32)
    mla_scale = MLA_QK ** -0.5
    mla_lat = attend(q_lat[:, :, :, None], k[:, :, :, None], v, one, mla_scale).reshape(b, s, BRANCH_WIDTH)

    dq = rms_norm(dq.reshape(b, t, DIFF_HEADS, 2, DIFF_DIM), p['g_q_diff'])
    dk = rms_norm(dk.reshape(b, t, DIFF_HEADS, 2, DIFF_DIM), p['g_k_diff'])
    dv = dv.reshape(b, t, DIFF_HEADS, DIFF_V)
    dk = jnp.concatenate([dk[:, :n_ctx], apply_rope2d(dk[:, n_ctx:], ang_diff)], axis=1)
    dq_lat = apply_rope2d(dq[:, n_ctx:], ang_diff)
    lv = p['lam'].astype(jnp.float32)
    lam = jnp.exp(jnp.sum(lv[0] * lv[1])) - jnp.exp(jnp.sum(lv[2] * lv[3])) + lam_init
    diff_mix = jnp.stack([jnp.ones_like(lam), -lam])
    diff_scale = DIFF_DIM ** -0.5

    def diff_out(o):
        return (rms_norm(o, p['g_subln']) * (1.0 - lam_init)).reshape(o.shape[0], o.shape[1], BRANCH_WIDTH)

    diff_lat = diff_out(attend(dq_lat, dk, dv, diff_mix, diff_scale))

    u = conv_c * conv_x
    conv_lat = conv_b[:, n_ctx:] * short_conv(u[:, n_ctx:], p['conv_w'])

    out_lat = merge([mla_lat, diff_lat, conv_lat], gate_logits[:, n_ctx:], p['w_br'], p['w_o'])
    if not need_ctx:
        return None, out_lat

    mla_ctx = attend(q[:, :n_ctx, :, None], k[:, :n_ctx, :, None], v[:, :n_ctx], one, mla_scale).reshape(b, n_ctx, BRANCH_WIDTH)
    diff_ctx = diff_out(attend(dq[:, :n_ctx], dk[:, :n_ctx], dv[:, :n_ctx], diff_mix, diff_scale))
    conv_ctx = conv_b[:, :n_ctx] * short_conv(u[:, :n_ctx], p['conv_w'])
    out_ctx = merge([mla_ctx, diff_ctx, conv_ctx], gate_logits[:, :n_ctx], p['w_br'], p['w_o'])
    return out_ctx, out_lat


def setup_inputs(seed: int = 0) -> dict:
    key = jax.random.key(seed)
    ks = iter(jax.random.split(key, 32))

    def normal(shape, scale):
        return scale * jax.random.normal(next(ks), shape, jnp.float32)

    def gain(shape):
        return 1.0 + 0.02 * jax.random.normal(next(ks), shape, jnp.float32)

    d, n = D_MODEL, DEPTH
    return {
        'x': normal((BATCH, SEQ, d), 1.0),
        'c': normal((BATCH, d), 1.0),
        'ctx': normal((BATCH, CTX_LEN, d), 1.0),
        'c_ctx': normal((d,), 1.0),
        'w_mod': normal((n, d, N_MOD * d), 0.5 * d ** -0.5),
        'b_mod': normal((n, N_MOD * d), 0.01),
        'norm_g': gain((n, 3, d)),
        'ffn1_up': normal((n, d, 2 * D_FF), d ** -0.5),
        'ffn1_down': normal((n, D_FF, d), D_FF ** -0.5),
        'ffn2_up': normal((n, d, 2 * D_FF), d ** -0.5),
        'ffn2_down': normal((n, D_FF, d), D_FF ** -0.5),
        'w_in': normal((n, d, IN_WIDTH), d ** -0.5),
        'g_cq': gain((n, MLA_Q_RANK)),
        'w_uq': normal((n, MLA_Q_RANK, MLA_HEADS * MLA_QK), MLA_Q_RANK ** -0.5),
        'g_ckv': gain((n, MLA_KV_RANK)),
        'w_ukv': normal((n, MLA_KV_RANK, MLA_HEADS * (MLA_NOPE + MLA_V)), MLA_KV_RANK ** -0.5),
        'g_q_mla': gain((n, MLA_QK)),
        'g_k_mla': gain((n, MLA_QK)),
        'g_q_diff': gain((n, DIFF_DIM)),
        'g_k_diff': gain((n, DIFF_DIM)),
        'lam': normal((n, 4, DIFF_DIM), 0.1),
        'g_subln': gain((n, DIFF_V)),
        'conv_w': normal((n, CONV_K, CONV_WIDTH), CONV_K ** -0.5),
        'w_br': normal((n, N_BRANCH, BRANCH_WIDTH, d), BRANCH_WIDTH ** -0.5),
        'w_o': normal((n, d, d), d ** -0.5),
    }


def reference(x, c, ctx, c_ctx, w_mod, b_mod, norm_g, ffn1_up, ffn1_down, ffn2_up, ffn2_down,
              w_in, g_cq, w_uq, g_ckv, w_ukv, g_q_mla, g_k_mla, g_q_diff, g_k_diff, lam,
              g_subln, conv_w, w_br, w_o):
    b, s, d = x.shape
    ang_mla = axial_angles(s, MLA_ROPE)
    ang_diff = axial_angles(s, DIFF_DIM)
    cond = jax.nn.silu(c)
    cond_ctx = jax.nn.silu(c_ctx)
    xl, xc = x, ctx
    for l in range(DEPTH):
        last = l == DEPTH - 1
        mod_l = (cond @ w_mod[l] + b_mod[l]).reshape(b, N_MOD, 1, d)
        mod_c = (cond_ctx @ w_mod[l] + b_mod[l]).reshape(N_MOD, d)
        ml = [mod_l[:, i] for i in range(N_MOD)]
        mc = [mod_c[i] for i in range(N_MOD)]

        xl = xl + 0.5 * ml[2] * swiglu(modulate(xl, norm_g[l, 0], ml[0], ml[1]), ffn1_up[l], ffn1_down[l])
        xc = xc + 0.5 * mc[2] * swiglu(modulate(xc, norm_g[l, 0], mc[0], mc[1]), ffn1_up[l], ffn1_down[l])

        hl = modulate(xl, norm_g[l, 1], ml[3], ml[4])
        hc = modulate(xc, norm_g[l, 1], mc[3], mc[4])
        p = {'w_in': w_in[l], 'g_cq': g_cq[l], 'w_uq': w_uq[l], 'g_ckv': g_ckv[l], 'w_ukv': w_ukv[l],
             'g_q_mla': g_q_mla[l], 'g_k_mla': g_k_mla[l], 'g_q_diff': g_q_diff[l], 'g_k_diff': g_k_diff[l],
             'lam': lam[l], 'g_subln': g_subln[l], 'conv_w': conv_w[l], 'w_br': w_br[l], 'w_o': w_o[l]}
        lam_init = 0.8 - 0.6 * math.exp(-0.3 * l)
        oc, ol = mixer(hc, hl, ang_mla, ang_diff, p, lam_init, not last)
        xl = xl + ml[5] * ol

        xl = xl + 0.5 * ml[8] * swiglu(modulate(xl, norm_g[l, 2], ml[6], ml[7]), ffn2_up[l], ffn2_down[l])
        if not last:
            xc = xc + mc[5] * oc
            xc = xc + 0.5 * mc[8] * swiglu(modulate(xc, norm_g[l, 2], mc[6], mc[7]), ffn2_up[l], ffn2_down[l])
    return xl
```

```python
import functools
import math

import jax
import jax.numpy as jnp
from jax import lax
from jax.experimental import pallas as pl
from jax.experimental.pallas import tpu as pltpu

GRID_W = 64
N_MOD = 9
MLA_HEADS = 8
MLA_NOPE = 64
MLA_ROPE = 32
MLA_QK = MLA_NOPE + MLA_ROPE
MLA_V = 64
DIFF_HEADS = 4
DIFF_DIM = 64
DIFF_V = 2 * DIFF_DIM
BRANCH_WIDTH = 512
ROPE_THETA = 10000.0
EPS = 1e-6

LANES = 128
SUBLANES = 8
VMEM_LIMIT_BYTES = 56 * 1024 * 1024
BF16 = jnp.bfloat16
F32 = jnp.float32


def _dot(a, b):
    return jnp.dot(a, b, preferred_element_type=F32)


def _dot_nt(a, b):
    return lax.dot_general(a, b, (((1,), (1,)), ((), ())), preferred_element_type=F32)


def _rms(x, g, n):
    ss = jnp.sum(x * x, axis=-1, keepdims=True)
    return x * lax.rsqrt(ss * (1.0 / n) + EPS) * g


def _modulate(x, g, shift, scale):
    return _rms(x, g, x.shape[-1]) * (1.0 + scale) + shift


def _sigmoid(x):
    return 1.0 / (1.0 + jnp.exp(-x))


def _rope(x, c, s1, s2, shift):
    return x * c + pltpu.roll(x, LANES - shift, 1) * s1 + pltpu.roll(x, shift, 1) * s2


def _params(sem):
    return pltpu.CompilerParams(dimension_semantics=sem, vmem_limit_bytes=VMEM_LIMIT_BYTES)


def _resident(block_shape, index_map):
    return pl.BlockSpec(block_shape, index_map, pipeline_mode=pl.Buffered(1))


def _mod_kernel(cond_ref, w_ref, b_ref, o_ref):
    c = cond_ref[...]
    a = (c * _sigmoid(c)).astype(BF16)
    o_ref[0] = _dot(a, w_ref[0].astype(BF16)) + b_ref[0]


def _mod_call(cond, w_mod, b_mod):
    depth, d, nd = w_mod.shape
    rows = cond.shape[0]
    tn = d
    return pl.pallas_call(
        _mod_kernel,
        grid=(depth, nd // tn),
        in_specs=[
            pl.BlockSpec((rows, d), lambda l, j: (0, 0)),
            pl.BlockSpec((1, d, tn), lambda l, j: (l, 0, j)),
            pl.BlockSpec((1, 1, tn), lambda l, j: (l, 0, j)),
        ],
        out_specs=pl.BlockSpec((1, rows, tn), lambda l, j: (l, 0, j)),
        out_shape=jax.ShapeDtypeStruct((depth, rows, nd), F32),
        compiler_params=_params(("arbitrary", "arbitrary")),
        name="mod",
    )(cond, w_mod, b_mod.reshape(depth, 1, nd))


def _ffn_kernel(x_ref, mod_ref, ng_ref, wup_ref, wdn_ref, o_ref, u_scr, *, k0, gi, dff, chunks):
    x = x_ref[...]
    md = mod_ref[0, 0]
    h = _modulate(x, ng_ref[0, gi:gi + 1], md[k0:k0 + 1], md[k0 + 1:k0 + 2]).astype(BF16)
    for lo, hi in chunks:
        a = _dot(h, wup_ref[0, :, lo:hi])
        b = _dot(h, wup_ref[0, :, dff + lo:dff + hi])
        u_scr[:, lo:hi] = (a * _sigmoid(a) * b).astype(BF16)
    y = _dot(u_scr[...], wdn_ref[0])
    o_ref[...] = x + (0.5 * md[k0 + 2:k0 + 3]) * y


def _ffn_call(xs, mod4, norm_g, wup, wdn, *, layer, k0, gi, n_tiles, tm, grp):
    d = xs.shape[1]
    dff = wdn.shape[1]
    half = (dff // 2 + 255) // 256 * 256
    chunks = ((0, half), (half, dff))
    kern = functools.partial(_ffn_kernel, k0=k0, gi=gi, dff=dff, chunks=chunks)
    return pl.pallas_call(
        kern,
        grid=(n_tiles,),
        in_specs=[
            pl.BlockSpec((tm, d), lambda i: (i, 0)),
            pl.BlockSpec((1, 1, N_MOD, d), lambda i: (layer, grp(i), 0, 0)),
            pl.BlockSpec((1, 3, d), lambda i: (layer, 0, 0)),
            _resident((1, d, 2 * dff), lambda i: (layer, 0, 0)),
            _resident((1, dff, d), lambda i: (layer, 0, 0)),
        ],
        out_specs=pl.BlockSpec((tm, d), lambda i: (i, 0)),
        out_shape=jax.ShapeDtypeStruct((n_tiles * tm, d), F32),
        scratch_shapes=[pltpu.VMEM((tm, dff), BF16)],
        compiler_params=_params(("arbitrary",)),
        name="ffn",
    )(xs, mod4, norm_g, wup, wdn)


_A_CQ = (0, 384)
_A_CKV = (384, 640)
_A_KR = (640, 768)
_A_DQ = (768, 1280)
_A_DK = (1280, 1792)
_A_DV = (1792, 2304)
_A_CC = (2304, 2816)
_A_CX = (2816, 3328)


def _inproj_kernel(x_ref, mod_ref, ng_ref, win_ref, gcq_ref, wuq_ref, gckv_ref, wkv_ref,
                   gqm_ref, gkm_ref, gqd_ref, gkd_ref,
                   cm_ref, s1m_ref, s2m_ref, cd_ref, s1d_ref, s2d_ref,
                   qm_ref, km_ref, vm_ref, dq_ref, dk_ref, dv_ref, u_ref):
    x = x_ref[...]
    md = mod_ref[0, 0]
    h = _modulate(x, ng_ref[0, 1:2], md[3:4], md[4:5]).astype(BF16)

    def proj(seg):
        return _dot(h, win_ref[0, :, seg[0]:seg[1]])

    cqn = _rms(proj(_A_CQ), gcq_ref[0], _A_CQ[1] - _A_CQ[0]).astype(BF16)
    q_raw = _dot(cqn, wuq_ref[0])
    ckvn = _rms(proj(_A_CKV), gckv_ref[0], _A_CKV[1] - _A_CKV[0]).astype(BF16)
    kv = _dot(ckvn, wkv_ref[0])
    krope = proj(_A_KR)
    cm, s1m, s2m = cm_ref[...], s1m_ref[...], s2m_ref[...]
    gqm, gkm = gqm_ref[0], gkm_ref[0]
    q_scale = MLA_QK ** -0.5
    shift_m = MLA_ROPE // 4
    for hd in range(MLA_HEADS):
        blk = slice(hd * LANES, (hd + 1) * LANES)
        qn = _rms(q_raw[:, blk], gqm, MLA_QK)
        qm_ref[:, blk] = (_rope(qn, cm, s1m, s2m, shift_m) * q_scale).astype(BF16)
        kn = _rms(kv[:, blk] + krope, gkm, MLA_QK)
        km_ref[:, blk] = _rope(kn, cm, s1m, s2m, shift_m).astype(BF16)
    vm_ref[...] = kv[:, MLA_HEADS * LANES:].astype(BF16)

    cd, s1d, s2d = cd_ref[...], s1d_ref[...], s2d_ref[...]
    lo = lax.broadcasted_iota(jnp.int32, (x.shape[0], LANES), 1) < DIFF_DIM
    shift_d = DIFF_DIM // 4

    def diff_heads(raw, g, scale, out_ref):
        for hd in range(DIFF_HEADS):
            blk = slice(hd * LANES, (hd + 1) * LANES)
            xh = raw[:, blk]
            sq = xh * xh
            s_lo = jnp.sum(jnp.where(lo, sq, 0.0), axis=-1, keepdims=True)
            s_hi = jnp.sum(jnp.where(lo, 0.0, sq), axis=-1, keepdims=True)
            r = lax.rsqrt(jnp.where(lo, s_lo, s_hi) * (1.0 / DIFF_DIM) + EPS)
            xr = _rope(xh * r * g, cd, s1d, s2d, shift_d)
            out_ref[:, blk] = (xr * scale).astype(BF16)

    diff_heads(proj(_A_DQ), gqd_ref[0], DIFF_DIM ** -0.5, dq_ref)
    diff_heads(proj(_A_DK), gkd_ref[0], 1.0, dk_ref)
    dv_ref[...] = proj(_A_DV).astype(BF16)

    u_ref[...] = proj(_A_CC) * proj(_A_CX)


def _inproj_call(xs, mod4, norm_g, wa, gcq, wuq, gckv, wkv, gqm, gkm, gqd, gkd, tabs,
                 *, layer, n_tiles, tm, grp, tab_blk):
    r, d = xs.shape
    na = wa.shape[2]

    def vec(n):
        return pl.BlockSpec((1, 1, n), lambda i: (layer, 0, 0))

    def tab():
        return pl.BlockSpec((tm, LANES), lambda i: (tab_blk(i), 0))

    def rows(n):
        return pl.BlockSpec((tm, n), lambda i: (i, 0))

    hm, hd = MLA_HEADS * LANES, DIFF_HEADS * LANES
    out_shapes = [
        jax.ShapeDtypeStruct((r, hm), BF16),
        jax.ShapeDtypeStruct((r, hm), BF16),
        jax.ShapeDtypeStruct((r, BRANCH_WIDTH), BF16),
        jax.ShapeDtypeStruct((r, hd), BF16),
        jax.ShapeDtypeStruct((r, hd), BF16),
        jax.ShapeDtypeStruct((r, hd), BF16),
        jax.ShapeDtypeStruct((r, BRANCH_WIDTH), F32),
    ]
    return pl.pallas_call(
        _inproj_kernel,
        grid=(n_tiles,),
        in_specs=[
            rows(d),
            pl.BlockSpec((1, 1, N_MOD, d), lambda i: (layer, grp(i), 0, 0)),
            pl.BlockSpec((1, 3, d), lambda i: (layer, 0, 0)),
            _resident((1, d, na), lambda i: (layer, 0, 0)),
            vec(gcq.shape[2]),
            _resident((1,) + wuq.shape[1:], lambda i: (layer, 0, 0)),
            vec(gckv.shape[2]),
            _resident((1,) + wkv.shape[1:], lambda i: (layer, 0, 0)),
            vec(LANES), vec(LANES), vec(LANES), vec(LANES),
            tab(), tab(), tab(), tab(), tab(), tab(),
        ],
        out_specs=[rows(s.shape[1]) for s in out_shapes],
        out_shape=out_shapes,
        compiler_params=_params(("arbitrary",)),
        name="inproj",
    )(xs, mod4, norm_g, wa, gcq, wuq, gckv, wkv, gqm, gkm, gqd, gkd, *tabs)


def _softmax_pv(q, k_refs, v_refs, kblk, vblk):
    s = [_dot_nt(q, k[:, kblk]) for k in k_refs]
    m = s[0].max(axis=-1, keepdims=True)
    for t in s[1:]:
        m = jnp.maximum(m, t.max(axis=-1, keepdims=True))
    e = [jnp.exp(t - m) for t in s]
    l = e[0].sum(axis=-1, keepdims=True)
    for t in e[1:]:
        l = l + t.sum(axis=-1, keepdims=True)
    o = _dot(e[0].astype(BF16), v_refs[0][:, vblk])
    for t, v in zip(e[1:], v_refs[1:]):
        o = o + _dot(t.astype(BF16), v[:, vblk])
    return o / l


def _mla_attn_kernel(*refs, n_seg):
    q_ref = refs[0]
    k_refs = refs[1:1 + n_seg]
    v_refs = refs[1 + n_seg:1 + 2 * n_seg]
    o_ref = refs[1 + 2 * n_seg]
    lo = lax.broadcasted_iota(jnp.int32, (q_ref.shape[0], LANES), 1) < MLA_V
    for pair in range(MLA_HEADS // 2):
        vblk = slice(pair * LANES, (pair + 1) * LANES)
        halves = []
        for hd in (2 * pair, 2 * pair + 1):
            kblk = slice(hd * LANES, (hd + 1) * LANES)
            halves.append(_softmax_pv(q_ref[:, kblk], k_refs, v_refs, kblk, vblk))
        o_ref[:, vblk] = jnp.where(lo, halves[0], halves[1]).astype(BF16)


def _diff_attn_kernel(*refs, n_seg, lam_init):
    q_ref = refs[0]
    k_refs = refs[1:1 + n_seg]
    v_refs = refs[1 + n_seg:1 + 2 * n_seg]
    lam_ref, gs_ref, o_ref = refs[1 + 2 * n_seg:]
    lv = lam_ref[0]
    lam = (jnp.exp(jnp.sum(lv[0:1] * lv[1:2], axis=-1, keepdims=True))
           - jnp.exp(jnp.sum(lv[2:3] * lv[3:4], axis=-1, keepdims=True)) + lam_init)
    gs = gs_ref[0]
    lo = lax.broadcasted_iota(jnp.int32, (q_ref.shape[0], LANES), 1) < DIFF_DIM
    for hd in range(DIFF_HEADS):
        blk = slice(hd * LANES, (hd + 1) * LANES)
        q = q_ref[:, blk]
        zero = jnp.zeros_like(q)
        o1 = _softmax_pv(jnp.where(lo, q, zero), k_refs, v_refs, blk, blk)
        o2 = _softmax_pv(jnp.where(lo, zero, q), k_refs, v_refs, blk, blk)
        o = o1 - lam * o2
        o_ref[:, blk] = (_rms(o, gs, DIFF_V) * (1.0 - lam_init)).astype(BF16)


def _attn_call(kern, q, k, v, extra, *, width_qk, width_v, batch, seq, ctx, r_lat, bq, latent, name):
    extra_specs = [pl.BlockSpec((1,) + a.shape[1:], functools.partial(lambda l, b, i: (l, 0, 0), lyr))
                   for a, lyr in extra]
    extra_args = [a for a, _ in extra]
    ctx_blk0 = r_lat // ctx
    if latent:
        nq = seq // bq
        grid = (batch, nq)
        q_spec = pl.BlockSpec((bq, width_qk), lambda b, i: (b * nq + i, 0))
        k_specs = [pl.BlockSpec((seq, width_qk), lambda b, i: (b, 0)),
                   pl.BlockSpec((ctx, width_qk), lambda b, i: (ctx_blk0 + b, 0))]
        v_specs = [pl.BlockSpec((seq, width_v), lambda b, i: (b, 0)),
                   pl.BlockSpec((ctx, width_v), lambda b, i: (ctx_blk0 + b, 0))]
        o_spec = pl.BlockSpec((bq, width_v), lambda b, i: (b * nq + i, 0))
        o_rows = r_lat
        k_args, v_args = [k, k], [v, v]
    else:
        grid = (batch, 1)
        q_spec = pl.BlockSpec((ctx, width_qk), lambda b, i: (ctx_blk0 + b, 0))
        k_specs = [pl.BlockSpec((ctx, width_qk), lambda b, i: (ctx_blk0 + b, 0))]
        v_specs = [pl.BlockSpec((ctx, width_v), lambda b, i: (ctx_blk0 + b, 0))]
        o_spec = pl.BlockSpec((ctx, width_v), lambda b, i: (b, 0))
        o_rows = batch * ctx
        k_args, v_args = [k], [v]
    return pl.pallas_call(
        functools.partial(kern, n_seg=len(k_args)),
        grid=grid,
        in_specs=[q_spec] + k_specs + v_specs + extra_specs,
        out_specs=o_spec,
        out_shape=jax.ShapeDtypeStruct((o_rows, width_v), BF16),
        compiler_params=_params(("arbitrary", "arbitrary")),
        name=name,
    )(q, *k_args, *v_args, *extra_args)


_B_GATE = 3 * 1024


def _merge_kernel(x_ref, mod_ref, ng_ref, wgc_ref, om_ref, od_ref, u_ref, up_ref, un_ref, cw_ref,
                  wbr_ref, wo_ref, o_ref, *, n_lat_tiles, seq, ctx):
    i = pl.program_id(0)
    tm, d = x_ref.shape
    x = x_ref[...]
    md = mod_ref[0, 0]
    h = _modulate(x, ng_ref[0, 1:2], md[3:4], md[4:5]).astype(BF16)

    u = u_ref[...]
    row = lax.broadcasted_iota(jnp.int32, (tm, 1), 0)
    seq_len = jnp.where(i < n_lat_tiles, seq, ctx)
    pos = (i * tm + row) & (seq_len - 1)
    u_dn = jnp.where(row == 0, up_ref[SUBLANES - 1:SUBLANES, :], pltpu.roll(u, 1, 0))
    u_dn = jnp.where(pos == 0, 0.0, u_dn)
    u_up = jnp.where(row == tm - 1, un_ref[0:1, :], pltpu.roll(u, tm - 1, 0))
    u_up = jnp.where(pos == seq_len - 1, 0.0, u_up)
    cw = cw_ref[0]
    y = cw[0:1] * u_dn + cw[1:2] * u + cw[2:3] * u_up
    conv = (_dot(h, wgc_ref[0, :, _B_GATE:]) * y).astype(BF16)

    branches = (om_ref[...], od_ref[...], conv)
    m = None
    for n, yb in enumerate(branches):
        g = _sigmoid(_dot(h, wgc_ref[0, :, n * d:(n + 1) * d]))
        t = g * _dot(yb, wbr_ref[0, n])
        m = t if m is None else m + t
    o_ref[...] = x + md[5:6] * _dot(m.astype(BF16), wo_ref[0])


def _merge_call(xs, mod4, norm_g, wgc, om, od, u, conv_w, wbr, wo, *, layer, n_tiles, tm, grp,
                n_lat_tiles, seq, ctx):
    r, d = xs.shape
    hb = tm // SUBLANES
    last8 = u.shape[0] // SUBLANES - 1

    def rows(n):
        return pl.BlockSpec((tm, n), lambda i: (i, 0))

    kern = functools.partial(_merge_kernel, n_lat_tiles=n_lat_tiles, seq=seq, ctx=ctx)
    return pl.pallas_call(
        kern,
        grid=(n_tiles,),
        in_specs=[
            rows(d),
            pl.BlockSpec((1, 1, N_MOD, d), lambda i: (layer, grp(i), 0, 0)),
            pl.BlockSpec((1, 3, d), lambda i: (layer, 0, 0)),
            _resident((1,) + wgc.shape[1:], lambda i: (layer, 0, 0)),
            rows(BRANCH_WIDTH), rows(BRANCH_WIDTH), rows(BRANCH_WIDTH),
            pl.BlockSpec((SUBLANES, BRANCH_WIDTH), lambda i: (jnp.maximum(i * hb - 1, 0), 0)),
            pl.BlockSpec((SUBLANES, BRANCH_WIDTH), lambda i: (jnp.minimum((i + 1) * hb, last8), 0)),
            pl.BlockSpec((1,) + conv_w.shape[1:], lambda i: (layer, 0, 0)),
            _resident((1,) + wbr.shape[1:], lambda i: (layer, 0, 0, 0)),
            _resident((1, d, d), lambda i: (layer, 0, 0)),
        ],
        out_specs=rows(d),
        out_shape=jax.ShapeDtypeStruct((n_tiles * tm, d), F32),
        compiler_params=_params(("arbitrary",)),
        name="merge",
    )(xs, mod4, norm_g, wgc, om, od, u, u, u, conv_w, wbr, wo)


def _rope_tables(seq, tm):
    t = jnp.arange(seq)
    row = (t // GRID_W).astype(F32)[:, None]
    col = (t % GRID_W).astype(F32)[:, None]

    lane = jnp.arange(LANES)

    def axis_tables(rot_dim, rel, live):
        q4 = rot_dim // 4
        inv = ROPE_THETA ** (-jnp.arange(q4, dtype=F32) / q4)
        ang = jnp.where((rel // (2 * q4)) == 0, row, col) * inv[rel % q4][None, :]
        first = ((rel // q4) % 2) == 0
        c = jnp.where(live, jnp.cos(ang), 1.0)
        s1 = jnp.where(live & first, -jnp.sin(ang), 0.0)
        s2 = jnp.where(live & ~first, jnp.sin(ang), 0.0)
        ident = [jnp.ones((tm, LANES), F32), jnp.zeros((tm, LANES), F32), jnp.zeros((tm, LANES), F32)]
        return [jnp.concatenate([a, b], axis=0) for a, b in zip((c, s1, s2), ident)]

    mla_live = (lane >= MLA_NOPE) & (lane < MLA_QK)
    mla = axis_tables(MLA_ROPE, jnp.where(mla_live, lane - MLA_NOPE, 0), mla_live)
    diff = axis_tables(DIFF_DIM, lane % DIFF_DIM, lane >= 0)
    return mla + diff


def _pad_last(a, n):
    return jnp.pad(a, [(0, 0)] * (a.ndim - 1) + [(0, n - a.shape[-1])])


def kernel(x, c, ctx, c_ctx, w_mod, b_mod, norm_g, ffn1_up, ffn1_down, ffn2_up, ffn2_down,
           w_in, g_cq, w_uq, g_ckv, w_ukv, g_q_mla, g_k_mla, g_q_diff, g_k_diff, lam,
           g_subln, conv_w, w_br, w_o):
    batch, seq, d = x.shape
    n_ctx = ctx.shape[1]
    depth = w_mod.shape[0]
    r_lat, r_ctx = batch * seq, batch * n_ctx
    tm = 512
    bq = 256
    assert seq % tm == 0 and r_ctx % tm == 0 and seq % bq == 0 and r_lat % n_ctx == 0
    assert seq & (seq - 1) == 0 and n_ctx & (n_ctx - 1) == 0 and seq % GRID_W == 0
    assert d == 1024 and w_br.shape[2] == BRANCH_WIDTH
    n_lat_tiles, n_all_tiles = r_lat // tm, (r_lat + r_ctx) // tm
    tiles_per_batch = seq // tm

    def grp(i):
        return jnp.where(i < n_lat_tiles, i // tiles_per_batch, batch)

    def tab_blk(i):
        return jnp.where(i < n_lat_tiles, i % tiles_per_batch, tiles_per_batch)

    rank_q, rank_kv = g_cq.shape[1], g_ckv.shape[1]
    o = [0]
    for n in (rank_q, rank_kv, MLA_ROPE, 512, 512, 512, 512, 512, 512, 3 * d):
        o.append(o[-1] + n)
    seg = lambda k: w_in[:, :, o[k]:o[k + 1]]
    zeros = lambda n: jnp.zeros((depth, d, n), w_in.dtype)
    wa = jnp.concatenate([seg(0), seg(1), zeros(MLA_NOPE), seg(2), zeros(LANES - MLA_QK),
                          seg(3), seg(4), seg(5), seg(7), seg(8)], axis=-1).astype(BF16)
    wgc = jnp.concatenate([seg(9), seg(6)], axis=-1).astype(BF16)
    wuq = _pad_last(w_uq.reshape(depth, rank_q, MLA_HEADS, MLA_QK), LANES)
    wuq = wuq.reshape(depth, rank_q, MLA_HEADS * LANES).astype(BF16)
    wkv4 = w_ukv.reshape(depth, rank_kv, MLA_HEADS, MLA_NOPE + MLA_V)
    wk = _pad_last(wkv4[..., :MLA_NOPE], LANES).reshape(depth, rank_kv, MLA_HEADS * LANES)
    wv = wkv4[..., MLA_NOPE:].reshape(depth, rank_kv, MLA_HEADS * MLA_V)
    wkv = jnp.concatenate([wk, wv], axis=-1).astype(BF16)
    ffn_w = [(ffn1_up.astype(BF16), ffn1_down.astype(BF16)), (ffn2_up.astype(BF16), ffn2_down.astype(BF16))]
    wbr = w_br.astype(BF16)
    wo = w_o.astype(BF16)
    gcq = g_cq[:, None, :]
    gckv = g_ckv[:, None, :]
    gqm = _pad_last(g_q_mla, LANES)[:, None, :]
    gkm = _pad_last(g_k_mla, LANES)[:, None, :]
    gqd = jnp.tile(g_q_diff, (1, 2))[:, None, :]
    gkd = jnp.tile(g_k_diff, (1, 2))[:, None, :]
    gsub = g_subln[:, None, :]
    tabs = _rope_tables(seq, tm)

    cond = jnp.concatenate([c, c_ctx[None, :]], axis=0)
    rows_pad = (batch + 1 + SUBLANES - 1) // SUBLANES * SUBLANES
    cond = jnp.pad(cond, ((0, rows_pad - batch - 1), (0, 0)))
    mod4 = _mod_call(cond, w_mod, b_mod).reshape(depth, rows_pad, N_MOD, d)

    xs = jnp.concatenate([x.reshape(r_lat, d), ctx.reshape(r_ctx, d)], axis=0)
    attn_kw = dict(batch=batch, seq=seq, ctx=n_ctx, r_lat=r_lat, bq=bq)
    for l in range(depth):
        last = l == depth - 1
        lam_init = 0.8 - 0.6 * math.exp(-0.3 * l)
        xs = _ffn_call(xs, mod4, norm_g, *ffn_w[0], layer=l, k0=0, gi=0,
                       n_tiles=n_all_tiles, tm=tm, grp=grp)
        qm, km, vm, dq, dk, dv, u = _inproj_call(
            xs, mod4, norm_g, wa, gcq, wuq, gckv, wkv, gqm, gkm, gqd, gkd, tabs,
            layer=l, n_tiles=n_all_tiles, tm=tm, grp=grp, tab_blk=tab_blk)
        diff_kern = functools.partial(_diff_attn_kernel, lam_init=lam_init)
        diff_extra = [(lam, l), (gsub, l)]
        om = [_attn_call(_mla_attn_kernel, qm, km, vm, [], width_qk=MLA_HEADS * LANES,
                         width_v=BRANCH_WIDTH, latent=True, name="mla_lat", **attn_kw)]
        od = [_attn_call(diff_kern, dq, dk, dv, diff_extra, width_qk=DIFF_HEADS * LANES,
                         width_v=BRANCH_WIDTH, latent=True, name="diff_lat", **attn_kw)]
        n_tiles = n_lat_tiles
        if not last:
            om.append(_attn_call(_mla_attn_kernel, qm, km, vm, [], width_qk=MLA_HEADS * LANES,
                                 width_v=BRANCH_WIDTH, latent=False, name="mla_ctx", **attn_kw))
            od.append(_attn_call(diff_kern, dq, dk, dv, diff_extra, width_qk=DIFF_HEADS * LANES,
                                 width_v=BRANCH_WIDTH, latent=False, name="diff_ctx", **attn_kw))
            n_tiles = n_all_tiles
        om = jnp.concatenate(om, axis=0) if len(om) > 1 else om[0]
        od = jnp.concatenate(od, axis=0) if len(od) > 1 else od[0]
        xs = _merge_call(xs, mod4, norm_g, wgc, om, od, u, conv_w, wbr, wo, layer=l, n_tiles=n_tiles,
                         tm=tm, grp=grp, n_lat_tiles=n_lat_tiles, seq=seq, ctx=n_ctx)
        xs = _ffn_call(xs, mod4, norm_g, *ffn_w[1], layer=l, k0=6, gi=2,
                       n_tiles=n_tiles, tm=tm, grp=grp)
    return xs.reshape(batch, seq, d)
```

```python
import functools
import math

import jax
import jax.numpy as jnp
from jax import lax
from jax.experimental import pallas as pl
from jax.experimental.pallas import tpu as pltpu

GRID_W = 64
N_MOD = 9
MLA_HEADS = 8
MLA_NOPE = 64
MLA_ROPE = 32
MLA_QK = MLA_NOPE + MLA_ROPE
MLA_V = 64
DIFF_HEADS = 4
DIFF_DIM = 64
DIFF_V = 2 * DIFF_DIM
BRANCH_WIDTH = 512
ROPE_THETA = 10000.0
EPS = 1e-6
LOG2E = math.log2(math.e)

LANES = 128
SUBLANES = 8
BF16_ROWS = 16
VMEM_LIMIT_BYTES = 56 * 1024 * 1024
BF16 = jnp.bfloat16
F32 = jnp.float32


def _dot(a, b):
    return jnp.dot(a, b, preferred_element_type=F32)


def _dot_nt(a, b):
    return lax.dot_general(a, b, (((1,), (1,)), ((), ())), preferred_element_type=F32)


def _rms(x, g, n):
    ss = jnp.sum(x * x, axis=-1, keepdims=True)
    return x * lax.rsqrt(ss * (1.0 / n) + EPS) * g


def _modulate(x, g, shift, scale):
    return _rms(x, g, x.shape[-1]) * (1.0 + scale) + shift


def _sigmoid(x):
    return 1.0 / (1.0 + jnp.exp(-x))


def _rope(x, c, s1, s2, shift):
    return x * c + pltpu.roll(x, LANES - shift, 1) * s1 + pltpu.roll(x, shift, 1) * s2


def _params(sem):
    return pltpu.CompilerParams(dimension_semantics=sem, vmem_limit_bytes=VMEM_LIMIT_BYTES)


def _resident(block_shape, index_map):
    return pl.BlockSpec(block_shape, index_map, pipeline_mode=pl.Buffered(1))


def _mod_kernel(cond_ref, w_ref, b_ref, o_ref):
    c = cond_ref[...]
    a = (c * _sigmoid(c)).astype(BF16)
    o_ref[0] = _dot(a, w_ref[0].astype(BF16)) + b_ref[0]


def _mod_call(cond, w_mod, b_mod):
    depth, d, nd = w_mod.shape
    rows = cond.shape[0]
    tn = d
    return pl.pallas_call(
        _mod_kernel,
        grid=(depth, nd // tn),
        in_specs=[
            pl.BlockSpec((rows, d), lambda l, j: (0, 0)),
            pl.BlockSpec((1, d, tn), lambda l, j: (l, 0, j)),
            pl.BlockSpec((1, 1, tn), lambda l, j: (l, 0, j)),
        ],
        out_specs=pl.BlockSpec((1, rows, tn), lambda l, j: (l, 0, j)),
        out_shape=jax.ShapeDtypeStruct((depth, rows, nd), F32),
        compiler_params=_params(("arbitrary", "arbitrary")),
        name="mod",
    )(cond, w_mod, b_mod.reshape(depth, 1, nd))


def _ffn_kernel(x_ref, mod_ref, ng_ref, wup_ref, wdn_ref, o_ref, u_scr, *, k0, gi, dff, chunks):
    x = x_ref[...]
    md = mod_ref[0, 0]
    h = _modulate(x, ng_ref[0, gi:gi + 1], md[k0:k0 + 1], md[k0 + 1:k0 + 2]).astype(BF16)
    for lo, hi in chunks:
        a = _dot(h, wup_ref[0, :, lo:hi])
        b = _dot(h, wup_ref[0, :, dff + lo:dff + hi])
        u_scr[:, lo:hi] = (a * _sigmoid(a) * b).astype(BF16)
    y = _dot(u_scr[...], wdn_ref[0])
    o_ref[...] = x + (0.5 * md[k0 + 2:k0 + 3]) * y


def _ffn_call(xs, mod4, norm_g, wup, wdn, *, layer, k0, gi, n_tiles, tm, grp):
    d = xs.shape[1]
    dff = wdn.shape[1]
    half = (dff // 2 + 255) // 256 * 256
    chunks = ((0, half), (half, dff))
    kern = functools.partial(_ffn_kernel, k0=k0, gi=gi, dff=dff, chunks=chunks)
    return pl.pallas_call(
        kern,
        grid=(n_tiles,),
        in_specs=[
            pl.BlockSpec((tm, d), lambda i: (i, 0)),
            pl.BlockSpec((1, 1, N_MOD, d), lambda i: (layer, grp(i), 0, 0)),
            pl.BlockSpec((1, 3, d), lambda i: (layer, 0, 0)),
            _resident((1, d, 2 * dff), lambda i: (layer, 0, 0)),
            _resident((1, dff, d), lambda i: (layer, 0, 0)),
        ],
        out_specs=pl.BlockSpec((tm, d), lambda i: (i, 0)),
        out_shape=jax.ShapeDtypeStruct((n_tiles * tm, d), F32),
        scratch_shapes=[pltpu.VMEM((tm, dff), BF16)],
        compiler_params=_params(("arbitrary",)),
        name="ffn",
    )(xs, mod4, norm_g, wup, wdn)


_A_CQ = (0, 384)
_A_CKV = (384, 640)
_A_KR = (640, 768)
_A_DQ = (768, 1280)
_A_DK = (1280, 1792)
_A_DV = (1792, 2304)
_A_CC = (2304, 2816)
_A_CX = (2816, 3328)


def _inproj_kernel(x_ref, mod_ref, ng_ref, win_ref, gcq_ref, wuq_ref, gckv_ref, wkv_ref,
                   gqm_ref, gkm_ref, gqd_ref, gkd_ref,
                   cm_ref, s1m_ref, s2m_ref, cd_ref, s1d_ref, s2d_ref,
                   qmt_ref, km_ref, vmt_ref, dqt_ref, dk_ref, dvt_ref, u_ref):
    x = x_ref[...]
    md = mod_ref[0, 0]
    h = _modulate(x, ng_ref[0, 1:2], md[3:4], md[4:5]).astype(BF16)

    def proj(seg):
        return _dot(h, win_ref[0, :, seg[0]:seg[1]])

    cqn = _rms(proj(_A_CQ), gcq_ref[0], _A_CQ[1] - _A_CQ[0]).astype(BF16)
    q_raw = _dot(cqn, wuq_ref[0])
    ckvn = _rms(proj(_A_CKV), gckv_ref[0], _A_CKV[1] - _A_CKV[0]).astype(BF16)
    kv = _dot(ckvn, wkv_ref[0])
    krope = proj(_A_KR)
    cm, s1m, s2m = cm_ref[...], s1m_ref[...], s2m_ref[...]
    gqm, gkm = gqm_ref[0], gkm_ref[0]
    q_scale = MLA_QK ** -0.5 * LOG2E
    shift_m = MLA_ROPE // 4
    for hd in range(MLA_HEADS):
        blk = slice(hd * LANES, (hd + 1) * LANES)
        qn = _rms(q_raw[:, blk], gqm, MLA_QK)
        qmt_ref[blk, :] = (_rope(qn, cm, s1m, s2m, shift_m) * q_scale).T.astype(BF16)
        kn = _rms(kv[:, blk] + krope, gkm, MLA_QK)
        km_ref[:, blk] = _rope(kn, cm, s1m, s2m, shift_m).astype(BF16)
    vmt_ref[...] = kv[:, MLA_HEADS * LANES:].T.astype(BF16)

    cd, s1d, s2d = cd_ref[...], s1d_ref[...], s2d_ref[...]
    lo = lax.broadcasted_iota(jnp.int32, (x.shape[0], LANES), 1) < DIFF_DIM
    shift_d = DIFF_DIM // 4

    def diff_heads(raw, g, scale, out_ref, transposed):
        for hd in range(DIFF_HEADS):
            blk = slice(hd * LANES, (hd + 1) * LANES)
            xh = raw[:, blk]
            sq = xh * xh
            s_lo = jnp.sum(jnp.where(lo, sq, 0.0), axis=-1, keepdims=True)
            s_hi = jnp.sum(jnp.where(lo, 0.0, sq), axis=-1, keepdims=True)
            r = lax.rsqrt(jnp.where(lo, s_lo, s_hi) * (1.0 / DIFF_DIM) + EPS)
            xr = _rope(xh * r * g, cd, s1d, s2d, shift_d) * scale
            if transposed:
                out_ref[blk, :] = xr.T.astype(BF16)
            else:
                out_ref[:, blk] = xr.astype(BF16)

    diff_heads(proj(_A_DQ), gqd_ref[0], DIFF_DIM ** -0.5 * LOG2E, dqt_ref, True)
    diff_heads(proj(_A_DK), gkd_ref[0], 1.0, dk_ref, False)
    dvt_ref[...] = proj(_A_DV).T.astype(BF16)

    u_ref[...] = proj(_A_CC) * proj(_A_CX)


def _inproj_call(xs, mod4, norm_g, wa, gcq, wuq, gckv, wkv, gqm, gkm, gqd, gkd, tabs,
                 *, layer, n_tiles, tm, grp, tab_blk):
    r, d = xs.shape
    na = wa.shape[2]

    def vec(n):
        return pl.BlockSpec((1, 1, n), lambda i: (layer, 0, 0))

    def tab():
        return pl.BlockSpec((tm, LANES), lambda i: (tab_blk(i), 0))

    def rows(n):
        return pl.BlockSpec((tm, n), lambda i: (i, 0))

    hm, hd = MLA_HEADS * LANES, DIFF_HEADS * LANES
    out_shapes = [
        jax.ShapeDtypeStruct((hm, r), BF16),
        jax.ShapeDtypeStruct((r, hm), BF16),
        jax.ShapeDtypeStruct((BRANCH_WIDTH, r), BF16),
        jax.ShapeDtypeStruct((hd, r), BF16),
        jax.ShapeDtypeStruct((r, hd), BF16),
        jax.ShapeDtypeStruct((hd, r), BF16),
        jax.ShapeDtypeStruct((r, BRANCH_WIDTH), F32),
    ]
    return pl.pallas_call(
        _inproj_kernel,
        grid=(n_tiles,),
        in_specs=[
            rows(d),
            pl.BlockSpec((1, 1, N_MOD, d), lambda i: (layer, grp(i), 0, 0)),
            pl.BlockSpec((1, 3, d), lambda i: (layer, 0, 0)),
            _resident((1, d, na), lambda i: (layer, 0, 0)),
            vec(gcq.shape[2]),
            _resident((1,) + wuq.shape[1:], lambda i: (layer, 0, 0)),
            vec(gckv.shape[2]),
            _resident((1,) + wkv.shape[1:], lambda i: (layer, 0, 0)),
            vec(LANES), vec(LANES), vec(LANES), vec(LANES),
            tab(), tab(), tab(), tab(), tab(), tab(),
        ],
        out_specs=[rows(s.shape[1]) if s.shape[0] == r else pl.BlockSpec((s.shape[0], tm), lambda i: (0, i))
                   for s in out_shapes],
        out_shape=out_shapes,
        compiler_params=_params(("arbitrary",)),
        name="inproj",
    )(xs, mod4, norm_g, wa, gcq, wuq, gckv, wkv, gqm, gkm, gqd, gkd, *tabs)


KEY_CHUNK = 256
LOGIT_SLOTS = 3


def _col_stat(x, op):
    t, n = x.shape
    r = 64 if t % 64 == 0 else SUBLANES
    return op(op(x.reshape(t // r, r, n), axis=0), axis=0, keepdims=True)


def _flash_heads(n_heads, get_qt, kblk, vrows, k_refs, vt_refs, s_scr, finish):
    depth = s_scr.shape[0]
    chunks = [(k_ref, vt_ref, c0) for k_ref, vt_ref in zip(k_refs, vt_refs)
              for c0 in range(0, k_ref.shape[0], KEY_CHUNK)]
    steps = [(h, ci) for h in range(n_heads) for ci in range(len(chunks))]
    ones = jnp.ones((BF16_ROWS, KEY_CHUNK), BF16)

    def logits(n):
        h, ci = steps[n]
        k_ref, _, c0 = chunks[ci]
        s_scr[n % depth] = _dot(k_ref[c0:c0 + KEY_CHUNK, kblk(h)], get_qt(h))

    for n in range(min(depth - 1, len(steps))):
        logits(n)
    m = acc = None
    for n, (h, ci) in enumerate(steps):
        if n + depth - 1 < len(steps):
            logits(n + depth - 1)
        _, vt_ref, c0 = chunks[ci]
        s = s_scr[n % depth]
        cm = _col_stat(s, jnp.max)
        m_new = cm if ci == 0 else jnp.maximum(m, cm)
        p = jnp.exp2(s - m_new).astype(BF16)
        vt1 = jnp.concatenate([vt_ref[vrows(h), c0:c0 + KEY_CHUNK], ones], axis=0)
        pv = _dot(vt1, p)
        acc = pv if ci == 0 else jnp.exp2(m - m_new) * acc + pv
        m = m_new
        if ci == len(chunks) - 1:
            dv = acc.shape[0] - BF16_ROWS
            finish(h, acc[:dv] * (1.0 / acc[dv:dv + 1]))


def _mla_attn_kernel(*refs, n_seg):
    qt_ref = refs[0]
    k_refs = refs[1:1 + n_seg]
    vt_refs = refs[1 + n_seg:1 + 2 * n_seg]
    o_ref, s_scr = refs[-2:]
    held = []

    def finish(h, ot):
        held.append(ot)
        if h % 2 == 1:
            pair_t = jnp.concatenate(held, axis=0)
            o_ref[:, (h // 2) * LANES:(h // 2 + 1) * LANES] = pair_t.T.astype(BF16)
            held.clear()

    _flash_heads(MLA_HEADS, lambda h: qt_ref[h * LANES:(h + 1) * LANES, :],
                 lambda h: slice(h * LANES, (h + 1) * LANES), lambda h: slice(h * MLA_V, (h + 1) * MLA_V),
                 k_refs, vt_refs, s_scr, finish)


def _diff_attn_kernel(*refs, n_seg, lam_init):
    qt_ref = refs[0]
    k_refs = refs[1:1 + n_seg]
    vt_refs = refs[1 + n_seg:1 + 2 * n_seg]
    lam_ref, gs_ref = refs[1 + 2 * n_seg:3 + 2 * n_seg]
    o_ref, s_scr = refs[-2:]
    bq = qt_ref.shape[1]
    lv = lam_ref[0]
    lam = (jnp.exp(jnp.sum(lv[0:1] * lv[1:2], axis=-1, keepdims=True))
           - jnp.exp(jnp.sum(lv[2:3] * lv[3:4], axis=-1, keepdims=True)) + lam_init)
    gs = gs_ref[0]
    lo = lax.broadcasted_iota(jnp.int32, (LANES, bq), 0) < DIFF_DIM
    blk = lambda h: slice(h * LANES, (h + 1) * LANES)

    def get_qt(h):
        qt = qt_ref[blk(h), :]
        zero = jnp.zeros_like(qt)
        return jnp.concatenate([jnp.where(lo, qt, zero), jnp.where(lo, zero, qt)], axis=1)

    def finish(h, oc):
        o = (oc[:, :bq] - lam * oc[:, bq:]).T
        o_ref[:, blk(h)] = (_rms(o, gs, DIFF_V) * (1.0 - lam_init)).astype(BF16)

    _flash_heads(DIFF_HEADS, get_qt, blk, blk, k_refs, vt_refs, s_scr, finish)


def _attn_call(kern, qt, k, vt, extra, prev, *, width_qk, width_v, q_cols, batch, seq, ctx, r_lat, bq,
               latent, name):
    extra_specs = [pl.BlockSpec((1,) + a.shape[1:], functools.partial(lambda l, b, i: (l, 0, 0), lyr))
                   for a, lyr in extra]
    extra_args = [a for a, _ in extra]
    ctx_blk0 = r_lat // ctx
    if latent:
        nq = seq // bq
        grid = (batch, nq)
        q_spec = pl.BlockSpec((width_qk, bq), lambda b, i: (0, b * nq + i))
        k_specs = [pl.BlockSpec((ctx, width_qk), lambda b, i: (ctx_blk0 + b, 0)),
                   pl.BlockSpec((seq, width_qk), lambda b, i: (b, 0))]
        v_specs = [pl.BlockSpec((width_v, ctx), lambda b, i: (0, ctx_blk0 + b)),
                   pl.BlockSpec((width_v, seq), lambda b, i: (0, b))]
        o_spec = pl.BlockSpec((bq, width_v), lambda b, i: (b * nq + i, 0))
        k_args, v_args = [k, k], [vt, vt]
        alias_specs, alias_args, aliases = [], [], {}
    else:
        grid = (batch, 1)
        q_spec = pl.BlockSpec((width_qk, ctx), lambda b, i: (0, ctx_blk0 + b))
        k_specs = [pl.BlockSpec((ctx, width_qk), lambda b, i: (ctx_blk0 + b, 0))]
        v_specs = [pl.BlockSpec((width_v, ctx), lambda b, i: (0, ctx_blk0 + b))]
        o_spec = pl.BlockSpec((ctx, width_v), lambda b, i: (ctx_blk0 + b, 0))
        k_args, v_args = [k], [vt]
        alias_specs, alias_args = [pl.BlockSpec(memory_space=pl.ANY)], [prev]
        aliases = {1 + 2 * len(k_args) + len(extra_args): 0}
    return pl.pallas_call(
        functools.partial(kern, n_seg=len(k_args)),
        grid=grid,
        in_specs=[q_spec] + k_specs + v_specs + extra_specs + alias_specs,
        out_specs=o_spec,
        out_shape=jax.ShapeDtypeStruct((k.shape[0], width_v), BF16),
        scratch_shapes=[pltpu.VMEM((LOGIT_SLOTS, KEY_CHUNK, q_cols * q_spec.block_shape[1]), F32)],
        input_output_aliases=aliases,
        compiler_params=_params(("arbitrary", "arbitrary")),
        name=name,
    )(qt, *k_args, *v_args, *extra_args, *alias_args)


_B_GATE = 3 * 1024


def _merge_kernel(x_ref, mod_ref, ng_ref, wgc_ref, om_ref, od_ref, u_ref, up_ref, un_ref, cw_ref,
                  wbr_ref, wo_ref, o_ref, *, n_lat_tiles, seq, ctx):
    i = pl.program_id(0)
    tm, d = x_ref.shape
    x = x_ref[...]
    md = mod_ref[0, 0]
    h = _modulate(x, ng_ref[0, 1:2], md[3:4], md[4:5]).astype(BF16)

    u = u_ref[...]
    row = lax.broadcasted_iota(jnp.int32, (tm, 1), 0)
    seq_len = jnp.where(i < n_lat_tiles, seq, ctx)
    pos = (i * tm + row) & (seq_len - 1)
    u_dn = jnp.where(row == 0, up_ref[SUBLANES - 1:SUBLANES, :], pltpu.roll(u, 1, 0))
    u_dn = jnp.where(pos == 0, 0.0, u_dn)
    u_up = jnp.where(row == tm - 1, un_ref[0:1, :], pltpu.roll(u, tm - 1, 0))
    u_up = jnp.where(pos == seq_len - 1, 0.0, u_up)
    cw = cw_ref[0]
    y = cw[0:1] * u_dn + cw[1:2] * u + cw[2:3] * u_up
    conv = (_dot(h, wgc_ref[0, :, _B_GATE:]) * y).astype(BF16)

    branches = (om_ref[...], od_ref[...], conv)
    m = None
    for n, yb in enumerate(branches):
        g = _sigmoid(_dot(h, wgc_ref[0, :, n * d:(n + 1) * d]))
        t = g * _dot(yb, wbr_ref[0, n])
        m = t if m is None else m + t
    o_ref[...] = x + md[5:6] * _dot(m.astype(BF16), wo_ref[0])


def _merge_call(xs, mod4, norm_g, wgc, om, od, u, conv_w, wbr, wo, *, layer, n_tiles, tm, grp,
                n_lat_tiles, seq, ctx):
    r, d = xs.shape
    hb = tm // SUBLANES
    last8 = u.shape[0] // SUBLANES - 1

    def rows(n):
        return pl.BlockSpec((tm, n), lambda i: (i, 0))

    kern = functools.partial(_merge_kernel, n_lat_tiles=n_lat_tiles, seq=seq, ctx=ctx)
    return pl.pallas_call(
        kern,
        grid=(n_tiles,),
        in_specs=[
            rows(d),
            pl.BlockSpec((1, 1, N_MOD, d), lambda i: (layer, grp(i), 0, 0)),
            pl.BlockSpec((1, 3, d), lambda i: (layer, 0, 0)),
            _resident((1,) + wgc.shape[1:], lambda i: (layer, 0, 0)),
            rows(BRANCH_WIDTH), rows(BRANCH_WIDTH), rows(BRANCH_WIDTH),
            pl.BlockSpec((SUBLANES, BRANCH_WIDTH), lambda i: (jnp.maximum(i * hb - 1, 0), 0)),
            pl.BlockSpec((SUBLANES, BRANCH_WIDTH), lambda i: (jnp.minimum((i + 1) * hb, last8), 0)),
            pl.BlockSpec((1,) + conv_w.shape[1:], lambda i: (layer, 0, 0)),
            _resident((1,) + wbr.shape[1:], lambda i: (layer, 0, 0, 0)),
            _resident((1, d, d), lambda i: (layer, 0, 0)),
        ],
        out_specs=rows(d),
        out_shape=jax.ShapeDtypeStruct((n_tiles * tm, d), F32),
        compiler_params=_params(("arbitrary",)),
        name="merge",
    )(xs, mod4, norm_g, wgc, om, od, u, u, u, conv_w, wbr, wo)


def _rope_tables(seq, tm):
    t = jnp.arange(seq)
    row = (t // GRID_W).astype(F32)[:, None]
    col = (t % GRID_W).astype(F32)[:, None]

    lane = jnp.arange(LANES)

    def axis_tables(rot_dim, rel, live):
        q4 = rot_dim // 4
        inv = ROPE_THETA ** (-jnp.arange(q4, dtype=F32) / q4)
        ang = jnp.where((rel // (2 * q4)) == 0, row, col) * inv[rel % q4][None, :]
        first = ((rel // q4) % 2) == 0
        c = jnp.where(live, jnp.cos(ang), 1.0)
        s1 = jnp.where(live & first, -jnp.sin(ang), 0.0)
        s2 = jnp.where(live & ~first, jnp.sin(ang), 0.0)
        ident = [jnp.ones((tm, LANES), F32), jnp.zeros((tm, LANES), F32), jnp.zeros((tm, LANES), F32)]
        return [jnp.concatenate([a, b], axis=0) for a, b in zip((c, s1, s2), ident)]

    mla_live = (lane >= MLA_NOPE) & (lane < MLA_QK)
    mla = axis_tables(MLA_ROPE, jnp.where(mla_live, lane - MLA_NOPE, 0), mla_live)
    diff = axis_tables(DIFF_DIM, lane % DIFF_DIM, lane >= 0)
    return mla + diff


def _pad_last(a, n):
    return jnp.pad(a, [(0, 0)] * (a.ndim - 1) + [(0, n - a.shape[-1])])


def kernel(x, c, ctx, c_ctx, w_mod, b_mod, norm_g, ffn1_up, ffn1_down, ffn2_up, ffn2_down,
           w_in, g_cq, w_uq, g_ckv, w_ukv, g_q_mla, g_k_mla, g_q_diff, g_k_diff, lam,
           g_subln, conv_w, w_br, w_o):
    batch, seq, d = x.shape
    n_ctx = ctx.shape[1]
    depth = w_mod.shape[0]
    r_lat, r_ctx = batch * seq, batch * n_ctx
    tm = 512
    bq_mla, bq_diff = 512, 256
    assert seq % tm == 0 and r_ctx % tm == 0 and seq % bq_mla == 0 and r_lat % n_ctx == 0
    assert n_ctx % KEY_CHUNK == 0 and seq % KEY_CHUNK == 0
    assert seq & (seq - 1) == 0 and n_ctx & (n_ctx - 1) == 0 and seq % GRID_W == 0
    assert d == 1024 and w_br.shape[2] == BRANCH_WIDTH
    n_lat_tiles, n_all_tiles = r_lat // tm, (r_lat + r_ctx) // tm
    tiles_per_batch = seq // tm

    def grp(i):
        return jnp.where(i < n_lat_tiles, i // tiles_per_batch, batch)

    def tab_blk(i):
        return jnp.where(i < n_lat_tiles, i % tiles_per_batch, tiles_per_batch)

    rank_q, rank_kv = g_cq.shape[1], g_ckv.shape[1]
    o = [0]
    for n in (rank_q, rank_kv, MLA_ROPE, 512, 512, 512, 512, 512, 512, 3 * d):
        o.append(o[-1] + n)
    seg = lambda k: w_in[:, :, o[k]:o[k + 1]]
    zeros = lambda n: jnp.zeros((depth, d, n), w_in.dtype)
    wa = jnp.concatenate([seg(0), seg(1), zeros(MLA_NOPE), seg(2), zeros(LANES - MLA_QK),
                          seg(3), seg(4), seg(5), seg(7), seg(8)], axis=-1).astype(BF16)
    wgc = jnp.concatenate([seg(9), seg(6)], axis=-1).astype(BF16)
    wuq = _pad_last(w_uq.reshape(depth, rank_q, MLA_HEADS, MLA_QK), LANES)
    wuq = wuq.reshape(depth, rank_q, MLA_HEADS * LANES).astype(BF16)
    wkv4 = w_ukv.reshape(depth, rank_kv, MLA_HEADS, MLA_NOPE + MLA_V)
    wk = _pad_last(wkv4[..., :MLA_NOPE], LANES).reshape(depth, rank_kv, MLA_HEADS * LANES)
    wv = wkv4[..., MLA_NOPE:].reshape(depth, rank_kv, MLA_HEADS * MLA_V)
    wkv = jnp.concatenate([wk, wv], axis=-1).astype(BF16)
    ffn_w = [(ffn1_up.astype(BF16), ffn1_down.astype(BF16)), (ffn2_up.astype(BF16), ffn2_down.astype(BF16))]
    wbr = w_br.astype(BF16)
    wo = w_o.astype(BF16)
    gcq = g_cq[:, None, :]
    gckv = g_ckv[:, None, :]
    gqm = _pad_last(g_q_mla, LANES)[:, None, :]
    gkm = _pad_last(g_k_mla, LANES)[:, None, :]
    gqd = jnp.tile(g_q_diff, (1, 2))[:, None, :]
    gkd = jnp.tile(g_k_diff, (1, 2))[:, None, :]
    gsub = g_subln[:, None, :]
    tabs = _rope_tables(seq, tm)

    cond = jnp.concatenate([c, c_ctx[None, :]], axis=0)
    rows_pad = (batch + 1 + SUBLANES - 1) // SUBLANES * SUBLANES
    cond = jnp.pad(cond, ((0, rows_pad - batch - 1), (0, 0)))
    mod4 = _mod_call(cond, w_mod, b_mod).reshape(depth, rows_pad, N_MOD, d)

    xs = jnp.concatenate([x.reshape(r_lat, d), ctx.reshape(r_ctx, d)], axis=0)
    attn_kw = dict(batch=batch, seq=seq, ctx=n_ctx, r_lat=r_lat)
    for l in range(depth):
        last = l == depth - 1
        lam_init = 0.8 - 0.6 * math.exp(-0.3 * l)
        xs = _ffn_call(xs, mod4, norm_g, *ffn_w[0], layer=l, k0=0, gi=0,
                       n_tiles=n_all_tiles, tm=tm, grp=grp)
        qmt, km, vmt, dqt, dk, dvt, u = _inproj_call(
            xs, mod4, norm_g, wa, gcq, wuq, gckv, wkv, gqm, gkm, gqd, gkd, tabs,
            layer=l, n_tiles=n_all_tiles, tm=tm, grp=grp, tab_blk=tab_blk)
        diff_kern = functools.partial(_diff_attn_kernel, lam_init=lam_init)
        diff_extra = [(lam, l), (gsub, l)]
        mla_kw = dict(width_qk=MLA_HEADS * LANES, width_v=BRANCH_WIDTH, q_cols=1, bq=bq_mla, **attn_kw)
        diff_kw = dict(width_qk=DIFF_HEADS * LANES, width_v=BRANCH_WIDTH, q_cols=2, bq=bq_diff, **attn_kw)
        om = _attn_call(_mla_attn_kernel, qmt, km, vmt, [], None, latent=True, name="mla_lat", **mla_kw)
        od = _attn_call(diff_kern, dqt, dk, dvt, diff_extra, None, latent=True, name="diff_lat", **diff_kw)
        n_tiles = n_lat_tiles
        if not last:
            om = _attn_call(_mla_attn_kernel, qmt, km, vmt, [], om, latent=False, name="mla_ctx", **mla_kw)
            od = _attn_call(diff_kern, dqt, dk, dvt, diff_extra, od, latent=False, name="diff_ctx", **diff_kw)
            n_tiles = n_all_tiles
        xs = _merge_call(xs, mod4, norm_g, wgc, om, od, u, conv_w, wbr, wo, layer=l, n_tiles=n_tiles,
                         tm=tm, grp=grp, n_lat_tiles=n_lat_tiles, seq=seq, ctx=n_ctx)
        xs = _ffn_call(xs, mod4, norm_g, *ffn_w[1], layer=l, k0=6, gi=2,
                       n_tiles=n_tiles, tm=tm, grp=grp)
    return xs.reshape(batch, seq, d)
```

```python
import functools
import math

import jax
import jax.numpy as jnp
from jax import lax
from jax.experimental import pallas as pl
from jax.experimental.pallas import tpu as pltpu

GRID_W = 64
N_MOD = 9
MLA_HEADS = 8
MLA_NOPE = 64
MLA_ROPE = 32
MLA_QK = MLA_NOPE + MLA_ROPE
MLA_V = 64
DIFF_HEADS = 4
DIFF_DIM = 64
DIFF_V = 2 * DIFF_DIM
BRANCH_WIDTH = 512
ROPE_THETA = 10000.0
EPS = 1e-6
LOG2E = math.log2(math.e)

LANES = 128
SUBLANES = 8
BF16_ROWS = 16
VMEM_LIMIT_BYTES = 56 * 1024 * 1024
BF16 = jnp.bfloat16
F32 = jnp.float32


def _dot(a, b):
    return jnp.dot(a, b, preferred_element_type=F32)


def _dot_nt(a, b):
    return lax.dot_general(a, b, (((1,), (1,)), ((), ())), preferred_element_type=F32)


def _rms(x, g, n):
    ss = jnp.sum(x * x, axis=-1, keepdims=True)
    return x * lax.rsqrt(ss * (1.0 / n) + EPS) * g


def _modulate(x, g, shift, scale):
    return _rms(x, g, x.shape[-1]) * (1.0 + scale) + shift


def _sigmoid(x):
    return 1.0 / (1.0 + jnp.exp(-x))


def _params(sem):
    return pltpu.CompilerParams(dimension_semantics=sem, vmem_limit_bytes=VMEM_LIMIT_BYTES)


def _resident(block_shape, index_map):
    return pl.BlockSpec(block_shape, index_map, pipeline_mode=pl.Buffered(1))


def _mod_kernel(cond_ref, w_ref, b_ref, o_ref):
    c = cond_ref[...]
    a = (c * _sigmoid(c)).astype(BF16)
    o_ref[0] = _dot(a, w_ref[0].astype(BF16)) + b_ref[0]


def _mod_call(cond, w_mod, b_mod):
    depth, d, nd = w_mod.shape
    rows = cond.shape[0]
    tn = d
    return pl.pallas_call(
        _mod_kernel,
        grid=(depth, nd // tn),
        in_specs=[
            pl.BlockSpec((rows, d), lambda l, j: (0, 0)),
            pl.BlockSpec((1, d, tn), lambda l, j: (l, 0, j)),
            pl.BlockSpec((1, 1, tn), lambda l, j: (l, 0, j)),
        ],
        out_specs=pl.BlockSpec((1, rows, tn), lambda l, j: (l, 0, j)),
        out_shape=jax.ShapeDtypeStruct((depth, rows, nd), F32),
        compiler_params=_params(("arbitrary", "arbitrary")),
        name="mod",
    )(cond, w_mod, b_mod.reshape(depth, 1, nd))


def _ffn_kernel(x_ref, mod_ref, ng_ref, wup_ref, wdn_ref, o_ref, u_scr, *, k0, gi, dff, chunks):
    x = x_ref[...]
    md = mod_ref[0, 0]
    h = _modulate(x, ng_ref[0, gi:gi + 1], md[k0:k0 + 1], md[k0 + 1:k0 + 2]).astype(BF16)
    for lo, hi in chunks:
        a = _dot(h, wup_ref[0, :, lo:hi])
        b = _dot(h, wup_ref[0, :, dff + lo:dff + hi])
        u_scr[:, lo:hi] = (a * _sigmoid(a) * b).astype(BF16)
    y = _dot(u_scr[...], wdn_ref[0])
    o_ref[...] = x + (0.5 * md[k0 + 2:k0 + 3]) * y


def _ffn_call(xs, mod4, norm_g, wup, wdn, *, layer, k0, gi, n_tiles, tm, grp):
    d = xs.shape[1]
    dff = wdn.shape[1]
    half = (dff // 2 + 255) // 256 * 256
    chunks = ((0, half), (half, dff))
    kern = functools.partial(_ffn_kernel, k0=k0, gi=gi, dff=dff, chunks=chunks)
    return pl.pallas_call(
        kern,
        grid=(n_tiles,),
        in_specs=[
            pl.BlockSpec((tm, d), lambda i: (i, 0)),
            pl.BlockSpec((1, 1, N_MOD, d), lambda i: (layer, grp(i), 0, 0)),
            pl.BlockSpec((1, 3, d), lambda i: (layer, 0, 0)),
            _resident((1, d, 2 * dff), lambda i: (layer, 0, 0)),
            _resident((1, dff, d), lambda i: (layer, 0, 0)),
        ],
        out_specs=pl.BlockSpec((tm, d), lambda i: (i, 0)),
        out_shape=jax.ShapeDtypeStruct((n_tiles * tm, d), F32),
        scratch_shapes=[pltpu.VMEM((tm, dff), BF16)],
        compiler_params=_params(("arbitrary",)),
        name="ffn",
    )(xs, mod4, norm_g, wup, wdn)


_T_CQ = (0, 384)
_T_CKV = (384, 640)
_T_DQ = (640, 1152)
_T_DV = (1152, 1664)
_N_CKV = (0, 256)
_N_KR = (256, 384)
_N_KR_SW = (384, 512)
_N_DK = (512, 1024)
_N_DK_SW = (1024, 1536)
_N_CC = (1536, 2048)
_N_CX = (2048, 2560)


def _tile_lanes(g, n):
    return jnp.concatenate([g] * (n // LANES), axis=1)


def _rms_t(x, g, n):
    ss = jnp.sum(x * x, axis=0, keepdims=True)
    return x * lax.rsqrt(ss * (1.0 / n) + EPS) * g


def _swap_row_blocks(x, width):
    parts = []
    for r0 in range(0, x.shape[0], 2 * width):
        parts += [x[r0 + width:r0 + 2 * width], x[r0:r0 + width]]
    return jnp.concatenate(parts, axis=0)


def _inproj_kernel(x_ref, mod_ref, ng_ref, wt_ref, wn_ref, gcqt_ref, wuqt_ref, gckvt_ref, wvt_ref,
                   gckv_ref, wk_ref, gqmt_ref, gkm_ref, gqdt_ref, gkd_ref,
                   cm_ref, sgm_ref, cd_ref, sgd_ref, cmt_ref, sgmt_ref, cdt_ref, sgdt_ref,
                   qmt_ref, km_ref, vmt_ref, dqt_ref, dk_ref, dvt_ref, u_ref):
    tm = x_ref.shape[0]
    x = x_ref[...]
    md = mod_ref[0, 0]
    hf = _modulate(x, ng_ref[0, 1:2], md[3:4], md[4:5])
    h = hf.astype(BF16)
    ht = hf.T.astype(BF16)

    pt = _dot(wt_ref[0], ht)
    pn = _dot(h, wn_ref[0])

    def rows(seg):
        return pt[seg[0]:seg[1]]

    def cols(seg):
        return pn[:, seg[0]:seg[1]]

    cqn_t = _rms_t(rows(_T_CQ), _tile_lanes(gcqt_ref[0], tm), _T_CQ[1] - _T_CQ[0]).astype(BF16)
    q_raw_t = _dot(wuqt_ref[0], cqn_t)
    ckvn_t = _rms_t(rows(_T_CKV), _tile_lanes(gckvt_ref[0], tm), _T_CKV[1] - _T_CKV[0]).astype(BF16)
    vmt_ref[...] = _dot(wvt_ref[0], ckvn_t).astype(BF16)
    dvt_ref[...] = rows(_T_DV).astype(BF16)

    cmt, sgmt = cmt_ref[...], sgmt_ref[...]
    gqmt = _tile_lanes(gqmt_ref[0], tm)
    q_scale = MLA_QK ** -0.5 * LOG2E
    for hd in range(MLA_HEADS):
        blk = slice(hd * LANES, (hd + 1) * LANES)
        qn = _rms_t(q_raw_t[blk], gqmt, MLA_QK)
        qr = qn * cmt + _swap_row_blocks(qn, MLA_ROPE // 4) * sgmt
        qmt_ref[blk, :] = (qr * q_scale).astype(BF16)

    cdt, sgdt = cdt_ref[...], sgdt_ref[...]
    gqdt = _tile_lanes(gqdt_ref[0], tm)
    d_scale = DIFF_DIM ** -0.5 * LOG2E
    dq_t = rows(_T_DQ)
    for hd in range(DIFF_HEADS):
        halves = []
        for comp in range(2):
            r0 = hd * LANES + comp * DIFF_DIM
            xc = dq_t[r0:r0 + DIFF_DIM]
            ss = jnp.sum(xc * xc, axis=0, keepdims=True)
            halves.append(xc * lax.rsqrt(ss * (1.0 / DIFF_DIM) + EPS))
        xn = jnp.concatenate(halves, axis=0) * gqdt
        xr = xn * cdt + _swap_row_blocks(xn, DIFF_DIM // 4) * sgdt
        dqt_ref[hd * LANES:(hd + 1) * LANES, :] = (xr * d_scale).astype(BF16)

    ckvn = _rms(cols(_N_CKV), gckv_ref[0], _N_CKV[1] - _N_CKV[0]).astype(BF16)
    k_raw = _dot(ckvn, wk_ref[0])
    krope = cols(_N_KR)
    gc_m = gkm_ref[0, 0:1] * cm_ref[...]
    rot_m = cols(_N_KR_SW) * (gkm_ref[0, 1:2] * sgm_ref[...])
    for hd in range(MLA_HEADS):
        blk = slice(hd * LANES, (hd + 1) * LANES)
        kf = k_raw[:, blk] + krope
        r = lax.rsqrt(jnp.sum(kf * kf, axis=-1, keepdims=True) * (1.0 / MLA_QK) + EPS)
        km_ref[:, blk] = ((kf * gc_m + rot_m) * r).astype(BF16)

    gc_d = gkd_ref[0, 0:1] * cd_ref[...]
    gs_d = gkd_ref[0, 1:2] * sgd_ref[...]
    lo = lax.broadcasted_iota(jnp.int32, (tm, LANES), 1) < DIFF_DIM
    dk, dk_sw = cols(_N_DK), cols(_N_DK_SW)
    for hd in range(DIFF_HEADS):
        blk = slice(hd * LANES, (hd + 1) * LANES)
        xh = dk[:, blk]
        sq = xh * xh
        s_lo = jnp.sum(jnp.where(lo, sq, 0.0), axis=-1, keepdims=True)
        s_hi = jnp.sum(jnp.where(lo, 0.0, sq), axis=-1, keepdims=True)
        r = lax.rsqrt(jnp.where(lo, s_lo, s_hi) * (1.0 / DIFF_DIM) + EPS)
        dk_ref[:, blk] = ((xh * gc_d + dk_sw[:, blk] * gs_d) * r).astype(BF16)

    u_ref[...] = cols(_N_CC) * cols(_N_CX)


def _inproj_call(xs, mod4, norm_g, wt, wn, gcqt, wuqt, gckvt, wvt, gckv, wk, gqmt, gkm, gqdt, gkd, tabs, tabs_t,
                 *, layer, n_tiles, tm, grp, tab_blk):
    r, d = xs.shape

    def vec(n):
        return pl.BlockSpec((1, 1, n), lambda i: (layer, 0, 0))

    def res(a):
        return _resident((1,) + a.shape[1:], lambda i: (layer, 0, 0))

    def tab():
        return pl.BlockSpec((tm, LANES), lambda i: (tab_blk(i), 0))

    def tab_t():
        return pl.BlockSpec((LANES, tm), lambda i: (0, tab_blk(i)))

    def rows(n):
        return pl.BlockSpec((tm, n), lambda i: (i, 0))

    hm, hd = MLA_HEADS * LANES, DIFF_HEADS * LANES
    out_shapes = [
        jax.ShapeDtypeStruct((hm, r), BF16),
        jax.ShapeDtypeStruct((r, hm), BF16),
        jax.ShapeDtypeStruct((BRANCH_WIDTH, r), BF16),
        jax.ShapeDtypeStruct((hd, r), BF16),
        jax.ShapeDtypeStruct((r, hd), BF16),
        jax.ShapeDtypeStruct((hd, r), BF16),
        jax.ShapeDtypeStruct((r, BRANCH_WIDTH), F32),
    ]
    return pl.pallas_call(
        _inproj_kernel,
        grid=(n_tiles,),
        in_specs=[
            rows(d),
            pl.BlockSpec((1, 1, N_MOD, d), lambda i: (layer, grp(i), 0, 0)),
            pl.BlockSpec((1, 3, d), lambda i: (layer, 0, 0)),
            res(wt), res(wn), res(gcqt), res(wuqt), res(gckvt), res(wvt),
            vec(gckv.shape[2]), res(wk), res(gqmt), res(gkm), res(gqdt), res(gkd),
            tab(), tab(), tab(), tab(),
            tab_t(), tab_t(), tab_t(), tab_t(),
        ],
        out_specs=[rows(s.shape[1]) if s.shape[0] == r else pl.BlockSpec((s.shape[0], tm), lambda i: (0, i))
                   for s in out_shapes],
        out_shape=out_shapes,
        compiler_params=_params(("arbitrary",)),
        name="inproj",
    )(xs, mod4, norm_g, wt, wn, gcqt, wuqt, gckvt, wvt, gckv, wk, gqmt, gkm, gqdt, gkd, *tabs, *tabs_t)


KEY_CHUNK = 256
LOGIT_SLOTS = 3


def _col_stat(x, op):
    t, n = x.shape
    r = 64 if t % 64 == 0 else SUBLANES
    return op(op(x.reshape(t // r, r, n), axis=0), axis=0, keepdims=True)


def _flash_heads(n_heads, get_qt, kblk, vrows, k_refs, vt_refs, s_scr, finish):
    depth = s_scr.shape[0]
    chunks = [(k_ref, vt_ref, c0) for k_ref, vt_ref in zip(k_refs, vt_refs)
              for c0 in range(0, k_ref.shape[0], KEY_CHUNK)]
    steps = [(h, ci) for h in range(n_heads) for ci in range(len(chunks))]
    ones = jnp.ones((BF16_ROWS, KEY_CHUNK), BF16)

    def logits(n):
        h, ci = steps[n]
        k_ref, _, c0 = chunks[ci]
        s_scr[n % depth] = _dot(k_ref[c0:c0 + KEY_CHUNK, kblk(h)], get_qt(h))

    for n in range(min(depth - 1, len(steps))):
        logits(n)
    m = acc = None
    for n, (h, ci) in enumerate(steps):
        if n + depth - 1 < len(steps):
            logits(n + depth - 1)
        _, vt_ref, c0 = chunks[ci]
        s = s_scr[n % depth]
        cm = _col_stat(s, jnp.max)
        m_new = cm if ci == 0 else jnp.maximum(m, cm)
        p = jnp.exp2(s - m_new).astype(BF16)
        vt1 = jnp.concatenate([vt_ref[vrows(h), c0:c0 + KEY_CHUNK], ones], axis=0)
        pv = _dot(vt1, p)
        acc = pv if ci == 0 else jnp.exp2(m - m_new) * acc + pv
        m = m_new
        if ci == len(chunks) - 1:
            dv = acc.shape[0] - BF16_ROWS
            finish(h, acc[:dv] * (1.0 / acc[dv:dv + 1]))


def _mla_attn_kernel(*refs, n_seg):
    qt_ref = refs[0]
    k_refs = refs[1:1 + n_seg]
    vt_refs = refs[1 + n_seg:1 + 2 * n_seg]
    o_ref, s_scr = refs[-2:]
    held = []

    def finish(h, ot):
        held.append(ot)
        if h % 2 == 1:
            pair_t = jnp.concatenate(held, axis=0)
            o_ref[:, (h // 2) * LANES:(h // 2 + 1) * LANES] = pair_t.T.astype(BF16)
            held.clear()

    _flash_heads(MLA_HEADS, lambda h: qt_ref[h * LANES:(h + 1) * LANES, :],
                 lambda h: slice(h * LANES, (h + 1) * LANES), lambda h: slice(h * MLA_V, (h + 1) * MLA_V),
                 k_refs, vt_refs, s_scr, finish)


def _diff_attn_kernel(*refs, n_seg, lam_init):
    qt_ref = refs[0]
    k_refs = refs[1:1 + n_seg]
    vt_refs = refs[1 + n_seg:1 + 2 * n_seg]
    lam_ref, gs_ref = refs[1 + 2 * n_seg:3 + 2 * n_seg]
    o_ref, s_scr = refs[-2:]
    bq = qt_ref.shape[1]
    lv = lam_ref[0]
    lam = (jnp.exp(jnp.sum(lv[0:1] * lv[1:2], axis=-1, keepdims=True))
           - jnp.exp(jnp.sum(lv[2:3] * lv[3:4], axis=-1, keepdims=True)) + lam_init)
    gs = gs_ref[0]
    lo = lax.broadcasted_iota(jnp.int32, (LANES, bq), 0) < DIFF_DIM
    blk = lambda h: slice(h * LANES, (h + 1) * LANES)

    def get_qt(h):
        qt = qt_ref[blk(h), :]
        zero = jnp.zeros_like(qt)
        return jnp.concatenate([jnp.where(lo, qt, zero), jnp.where(lo, zero, qt)], axis=1)

    def finish(h, oc):
        o = (oc[:, :bq] - lam * oc[:, bq:]).T
        o_ref[:, blk(h)] = (_rms(o, gs, DIFF_V) * (1.0 - lam_init)).astype(BF16)

    _flash_heads(DIFF_HEADS, get_qt, blk, blk, k_refs, vt_refs, s_scr, finish)


def _attn_call(kern, qt, k, vt, extra, prev, *, width_qk, width_v, q_cols, batch, seq, ctx, r_lat, bq,
               latent, name):
    extra_specs = [pl.BlockSpec((1,) + a.shape[1:], functools.partial(lambda l, b, i: (l, 0, 0), lyr))
                   for a, lyr in extra]
    extra_args = [a for a, _ in extra]
    ctx_blk0 = r_lat // ctx
    if latent:
        nq = seq // bq
        grid = (batch, nq)
        q_spec = pl.BlockSpec((width_qk, bq), lambda b, i: (0, b * nq + i))
        k_specs = [pl.BlockSpec((ctx, width_qk), lambda b, i: (ctx_blk0 + b, 0)),
                   pl.BlockSpec((seq, width_qk), lambda b, i: (b, 0))]
        v_specs = [pl.BlockSpec((width_v, ctx), lambda b, i: (0, ctx_blk0 + b)),
                   pl.BlockSpec((width_v, seq), lambda b, i: (0, b))]
        o_spec = pl.BlockSpec((bq, width_v), lambda b, i: (b * nq + i, 0))
        k_args, v_args = [k, k], [vt, vt]
        alias_specs, alias_args, aliases = [], [], {}
    else:
        grid = (batch, 1)
        q_spec = pl.BlockSpec((width_qk, ctx), lambda b, i: (0, ctx_blk0 + b))
        k_specs = [pl.BlockSpec((ctx, width_qk), lambda b, i: (ctx_blk0 + b, 0))]
        v_specs = [pl.BlockSpec((width_v, ctx), lambda b, i: (0, ctx_blk0 + b))]
        o_spec = pl.BlockSpec((ctx, width_v), lambda b, i: (ctx_blk0 + b, 0))
        k_args, v_args = [k], [vt]
        alias_specs, alias_args = [pl.BlockSpec(memory_space=pl.ANY)], [prev]
        aliases = {1 + 2 * len(k_args) + len(extra_args): 0}
    return pl.pallas_call(
        functools.partial(kern, n_seg=len(k_args)),
        grid=grid,
        in_specs=[q_spec] + k_specs + v_specs + extra_specs + alias_specs,
        out_specs=o_spec,
        out_shape=jax.ShapeDtypeStruct((k.shape[0], width_v), BF16),
        scratch_shapes=[pltpu.VMEM((LOGIT_SLOTS, KEY_CHUNK, q_cols * q_spec.block_shape[1]), F32)],
        input_output_aliases=aliases,
        compiler_params=_params(("arbitrary", "arbitrary")),
        name=name,
    )(qt, *k_args, *v_args, *extra_args, *alias_args)


_B_GATE = 3 * 1024


def _merge_kernel(x_ref, mod_ref, ng_ref, wgc_ref, om_ref, od_ref, u_ref, up_ref, un_ref, cw_ref,
                  wbr_ref, wo_ref, o_ref, *, n_lat_tiles, seq, ctx):
    i = pl.program_id(0)
    tm, d = x_ref.shape
    x = x_ref[...]
    md = mod_ref[0, 0]
    h = _modulate(x, ng_ref[0, 1:2], md[3:4], md[4:5]).astype(BF16)

    u = u_ref[...]
    row = lax.broadcasted_iota(jnp.int32, (tm, 1), 0)
    seq_len = jnp.where(i < n_lat_tiles, seq, ctx)
    pos = (i * tm + row) & (seq_len - 1)
    u_dn = jnp.where(row == 0, up_ref[SUBLANES - 1:SUBLANES, :], pltpu.roll(u, 1, 0))
    u_dn = jnp.where(pos == 0, 0.0, u_dn)
    u_up = jnp.where(row == tm - 1, un_ref[0:1, :], pltpu.roll(u, tm - 1, 0))
    u_up = jnp.where(pos == seq_len - 1, 0.0, u_up)
    cw = cw_ref[0]
    y = cw[0:1] * u_dn + cw[1:2] * u + cw[2:3] * u_up
    conv = (_dot(h, wgc_ref[0, :, _B_GATE:]) * y).astype(BF16)

    branches = (om_ref[...], od_ref[...], conv)
    m = None
    for n, yb in enumerate(branches):
        g = _sigmoid(_dot(h, wgc_ref[0, :, n * d:(n + 1) * d]))
        t = g * _dot(yb, wbr_ref[0, n])
        m = t if m is None else m + t
    o_ref[...] = x + md[5:6] * _dot(m.astype(BF16), wo_ref[0])


def _merge_call(xs, mod4, norm_g, wgc, om, od, u, conv_w, wbr, wo, *, layer, n_tiles, tm, grp,
                n_lat_tiles, seq, ctx):
    r, d = xs.shape
    hb = tm // SUBLANES
    last8 = u.shape[0] // SUBLANES - 1

    def rows(n):
        return pl.BlockSpec((tm, n), lambda i: (i, 0))

    kern = functools.partial(_merge_kernel, n_lat_tiles=n_lat_tiles, seq=seq, ctx=ctx)
    return pl.pallas_call(
        kern,
        grid=(n_tiles,),
        in_specs=[
            rows(d),
            pl.BlockSpec((1, 1, N_MOD, d), lambda i: (layer, grp(i), 0, 0)),
            pl.BlockSpec((1, 3, d), lambda i: (layer, 0, 0)),
            _resident((1,) + wgc.shape[1:], lambda i: (layer, 0, 0)),
            rows(BRANCH_WIDTH), rows(BRANCH_WIDTH), rows(BRANCH_WIDTH),
            pl.BlockSpec((SUBLANES, BRANCH_WIDTH), lambda i: (jnp.maximum(i * hb - 1, 0), 0)),
            pl.BlockSpec((SUBLANES, BRANCH_WIDTH), lambda i: (jnp.minimum((i + 1) * hb, last8), 0)),
            pl.BlockSpec((1,) + conv_w.shape[1:], lambda i: (layer, 0, 0)),
            _resident((1,) + wbr.shape[1:], lambda i: (layer, 0, 0, 0)),
            _resident((1, d, d), lambda i: (layer, 0, 0)),
        ],
        out_specs=rows(d),
        out_shape=jax.ShapeDtypeStruct((n_tiles * tm, d), F32),
        compiler_params=_params(("arbitrary",)),
        name="merge",
    )(xs, mod4, norm_g, wgc, om, od, u, u, u, conv_w, wbr, wo)


def _rope_tables(seq, tm):
    t = jnp.arange(seq)
    row = (t // GRID_W).astype(F32)[:, None]
    col = (t % GRID_W).astype(F32)[:, None]

    lane = jnp.arange(LANES)

    def axis_tables(rot_dim, rel, live):
        q4 = rot_dim // 4
        inv = ROPE_THETA ** (-jnp.arange(q4, dtype=F32) / q4)
        ang = jnp.where((rel // (2 * q4)) == 0, row, col) * inv[rel % q4][None, :]
        first = ((rel // q4) % 2) == 0
        c = jnp.where(live, jnp.cos(ang), 1.0)
        s = jnp.where(live, jnp.where(first, -jnp.sin(ang), jnp.sin(ang)), 0.0)
        ident = [jnp.ones((tm, LANES), F32), jnp.zeros((tm, LANES), F32)]
        return [jnp.concatenate([a, b], axis=0) for a, b in zip((c, s), ident)]

    mla_live = (lane >= MLA_NOPE) & (lane < MLA_QK)
    mla = axis_tables(MLA_ROPE, jnp.where(mla_live, lane - MLA_NOPE, 0), mla_live)
    diff = axis_tables(DIFF_DIM, lane % DIFF_DIM, lane >= 0)
    return mla + diff


def _pad_last(a, n):
    return jnp.pad(a, [(0, 0)] * (a.ndim - 1) + [(0, n - a.shape[-1])])


def kernel(x, c, ctx, c_ctx, w_mod, b_mod, norm_g, ffn1_up, ffn1_down, ffn2_up, ffn2_down,
           w_in, g_cq, w_uq, g_ckv, w_ukv, g_q_mla, g_k_mla, g_q_diff, g_k_diff, lam,
           g_subln, conv_w, w_br, w_o):
    batch, seq, d = x.shape
    n_ctx = ctx.shape[1]
    depth = w_mod.shape[0]
    r_lat, r_ctx = batch * seq, batch * n_ctx
    tm = 512
    bq_mla, bq_diff = 512, 256
    assert seq % tm == 0 and r_ctx % tm == 0 and seq % bq_mla == 0 and r_lat % n_ctx == 0
    assert n_ctx % KEY_CHUNK == 0 and seq % KEY_CHUNK == 0
    assert seq & (seq - 1) == 0 and n_ctx & (n_ctx - 1) == 0 and seq % GRID_W == 0
    assert d == 1024 and w_br.shape[2] == BRANCH_WIDTH
    n_lat_tiles, n_all_tiles = r_lat // tm, (r_lat + r_ctx) // tm
    tiles_per_batch = seq // tm

    def grp(i):
        return jnp.where(i < n_lat_tiles, i // tiles_per_batch, batch)

    def tab_blk(i):
        return jnp.where(i < n_lat_tiles, i % tiles_per_batch, tiles_per_batch)

    rank_q, rank_kv = g_cq.shape[1], g_ckv.shape[1]
    o = [0]
    for n in (rank_q, rank_kv, MLA_ROPE, 512, 512, 512, 512, 512, 512, 3 * d):
        o.append(o[-1] + n)
    seg = lambda k: w_in[:, :, o[k]:o[k + 1]]
    zeros = lambda n: jnp.zeros((depth, d, n), w_in.dtype)
    swap = lambda a: jnp.swapaxes(a, 1, 2)
    lanes = lambda g: jnp.broadcast_to(g[:, :, None], g.shape + (LANES,))
    wt = swap(jnp.concatenate([seg(0), seg(1), seg(3), seg(5)], axis=-1)).astype(BF16)
    def partner(a, rot_dim):
        q4 = rot_dim // 4
        g = a.reshape(a.shape[:-1] + (a.shape[-1] // rot_dim, 2, 2, q4))
        return jnp.flip(g, axis=-2).reshape(a.shape)

    def place_rope(a):
        return jnp.concatenate([jnp.zeros(a.shape[:-1] + (MLA_NOPE,), a.dtype), a,
                                jnp.zeros(a.shape[:-1] + (LANES - MLA_QK,), a.dtype)], axis=-1)

    wn = jnp.concatenate([seg(1), place_rope(seg(2)), place_rope(partner(seg(2), MLA_ROPE)),
                          seg(4), partner(seg(4), DIFF_DIM), seg(7), seg(8)], axis=-1).astype(BF16)
    wgc = jnp.concatenate([seg(9), seg(6)], axis=-1).astype(BF16)
    wuq = _pad_last(w_uq.reshape(depth, rank_q, MLA_HEADS, MLA_QK), LANES)
    wuqt = swap(wuq.reshape(depth, rank_q, MLA_HEADS * LANES)).astype(BF16)
    wkv4 = w_ukv.reshape(depth, rank_kv, MLA_HEADS, MLA_NOPE + MLA_V)
    wk = _pad_last(wkv4[..., :MLA_NOPE], LANES).reshape(depth, rank_kv, MLA_HEADS * LANES).astype(BF16)
    wvt = swap(wkv4[..., MLA_NOPE:].reshape(depth, rank_kv, MLA_HEADS * MLA_V)).astype(BF16)
    ffn_w = [(ffn1_up.astype(BF16), ffn1_down.astype(BF16)), (ffn2_up.astype(BF16), ffn2_down.astype(BF16))]
    wbr = w_br.astype(BF16)
    wo = w_o.astype(BF16)
    gcqt = lanes(g_cq)
    gckvt = lanes(g_ckv)
    gckv = g_ckv[:, None, :]
    gqmt = lanes(_pad_last(g_q_mla, LANES))
    gkm = jnp.stack([_pad_last(g_k_mla, LANES),
                     place_rope(partner(g_k_mla[:, MLA_NOPE:], MLA_ROPE))], axis=1)
    gqdt = lanes(jnp.tile(g_q_diff, (1, 2)))
    gkd = jnp.stack([jnp.tile(g_k_diff, (1, 2)), jnp.tile(partner(g_k_diff, DIFF_DIM), (1, 2))], axis=1)
    gsub = g_subln[:, None, :]
    tabs = _rope_tables(seq, tm)
    tabs_t = [t.T for t in tabs]

    cond = jnp.concatenate([c, c_ctx[None, :]], axis=0)
    rows_pad = (batch + 1 + SUBLANES - 1) // SUBLANES * SUBLANES
    cond = jnp.pad(cond, ((0, rows_pad - batch - 1), (0, 0)))
    mod4 = _mod_call(cond, w_mod, b_mod).reshape(depth, rows_pad, N_MOD, d)

    xs = jnp.concatenate([x.reshape(r_lat, d), ctx.reshape(r_ctx, d)], axis=0)
    attn_kw = dict(batch=batch, seq=seq, ctx=n_ctx, r_lat=r_lat)
    for l in range(depth):
        last = l == depth - 1
        lam_init = 0.8 - 0.6 * math.exp(-0.3 * l)
        xs = _ffn_call(xs, mod4, norm_g, *ffn_w[0], layer=l, k0=0, gi=0,
                       n_tiles=n_all_tiles, tm=tm, grp=grp)
        qmt, km, vmt, dqt, dk, dvt, u = _inproj_call(
            xs, mod4, norm_g, wt, wn, gcqt, wuqt, gckvt, wvt, gckv, wk, gqmt, gkm, gqdt, gkd, tabs, tabs_t,
            layer=l, n_tiles=n_all_tiles, tm=tm, grp=grp, tab_blk=tab_blk)
        diff_kern = functools.partial(_diff_attn_kernel, lam_init=lam_init)
        diff_extra = [(lam, l), (gsub, l)]
        mla_kw = dict(width_qk=MLA_HEADS * LANES, width_v=BRANCH_WIDTH, q_cols=1, bq=bq_mla, **attn_kw)
        diff_kw = dict(width_qk=DIFF_HEADS * LANES, width_v=BRANCH_WIDTH, q_cols=2, bq=bq_diff, **attn_kw)
        om = _attn_call(_mla_attn_kernel, qmt, km, vmt, [], None, latent=True, name="mla_lat", **mla_kw)
        od = _attn_call(diff_kern, dqt, dk, dvt, diff_extra, None, latent=True, name="diff_lat", **diff_kw)
        n_tiles = n_lat_tiles
        if not last:
            om = _attn_call(_mla_attn_kernel, qmt, km, vmt, [], om, latent=False, name="mla_ctx", **mla_kw)
            od = _attn_call(diff_kern, dqt, dk, dvt, diff_extra, od, latent=False, name="diff_ctx", **diff_kw)
            n_tiles = n_all_tiles
        xs = _merge_call(xs, mod4, norm_g, wgc, om, od, u, conv_w, wbr, wo, layer=l, n_tiles=n_tiles,
                         tm=tm, grp=grp, n_lat_tiles=n_lat_tiles, seq=seq, ctx=n_ctx)
        xs = _ffn_call(xs, mod4, norm_g, *ffn_w[1], layer=l, k0=6, gi=2,
                       n_tiles=n_tiles, tm=tm, grp=grp)
    return xs.reshape(batch, seq, d)
```

```python
import functools
import math

import jax
import jax.numpy as jnp
from jax import lax
from jax.experimental import pallas as pl
from jax.experimental.pallas import tpu as pltpu

GRID_W = 64
N_MOD = 9
MLA_HEADS = 8
MLA_NOPE = 64
MLA_ROPE = 32
MLA_QK = MLA_NOPE + MLA_ROPE
MLA_V = 64
DIFF_HEADS = 4
DIFF_DIM = 64
DIFF_V = 2 * DIFF_DIM
BRANCH_WIDTH = 512
ROPE_THETA = 10000.0
EPS = 1e-6
LOG2E = math.log2(math.e)

LANES = 128
SUBLANES = 8
BF16_ROWS = 16
VMEM_LIMIT_BYTES = 56 * 1024 * 1024
BF16 = jnp.bfloat16
F32 = jnp.float32


def _dot(a, b):
    return jnp.dot(a, b, preferred_element_type=F32)


def _dot_nt(a, b):
    return lax.dot_general(a, b, (((1,), (1,)), ((), ())), preferred_element_type=F32)


def _rms(x, g, n):
    ss = jnp.sum(x * x, axis=-1, keepdims=True)
    return x * lax.rsqrt(ss * (1.0 / n) + EPS) * g


def _modulate(x, g, shift, scale):
    return _rms(x, g, x.shape[-1]) * (1.0 + scale) + shift


def _sigmoid(x):
    return 1.0 / (1.0 + jnp.exp(-x))


def _params(sem):
    return pltpu.CompilerParams(dimension_semantics=sem, vmem_limit_bytes=VMEM_LIMIT_BYTES)


def _resident(block_shape, index_map):
    return pl.BlockSpec(block_shape, index_map, pipeline_mode=pl.Buffered(1))


def _mod_kernel(cond_ref, w_ref, b_ref, o_ref):
    c = cond_ref[...]
    a = (c * _sigmoid(c)).astype(BF16)
    o_ref[0] = _dot(a, w_ref[0].astype(BF16)) + b_ref[0]


def _mod_call(cond, w_mod, b_mod):
    depth, d, nd = w_mod.shape
    rows = cond.shape[0]
    tn = d
    return pl.pallas_call(
        _mod_kernel,
        grid=(depth, nd // tn),
        in_specs=[
            pl.BlockSpec((rows, d), lambda l, j: (0, 0)),
            pl.BlockSpec((1, d, tn), lambda l, j: (l, 0, j)),
            pl.BlockSpec((1, 1, tn), lambda l, j: (l, 0, j)),
        ],
        out_specs=pl.BlockSpec((1, rows, tn), lambda l, j: (l, 0, j)),
        out_shape=jax.ShapeDtypeStruct((depth, rows, nd), F32),
        compiler_params=_params(("arbitrary", "arbitrary")),
        name="mod",
    )(cond, w_mod, b_mod.reshape(depth, 1, nd))


def _ffn_kernel(x_ref, mod_ref, ng_ref, wup_ref, wdn_ref, o_ref, u_scr, *, k0, gi, dff, chunks):
    x = x_ref[...]
    md = mod_ref[0, 0]
    h = _modulate(x, ng_ref[0, gi:gi + 1], md[k0:k0 + 1], md[k0 + 1:k0 + 2]).astype(BF16)
    for lo, hi in chunks:
        a = _dot(h, wup_ref[0, :, lo:hi])
        b = _dot(h, wup_ref[0, :, dff + lo:dff + hi])
        u_scr[:, lo:hi] = (a * _sigmoid(a) * b).astype(BF16)
    y = _dot(u_scr[...], wdn_ref[0])
    o_ref[...] = x + (0.5 * md[k0 + 2:k0 + 3]) * y


def _ffn_call(xs, mod4, norm_g, wup, wdn, *, layer, k0, gi, n_tiles, tm, grp):
    d = xs.shape[1]
    dff = wdn.shape[1]
    half = (dff // 2 + 255) // 256 * 256
    chunks = ((0, half), (half, dff))
    kern = functools.partial(_ffn_kernel, k0=k0, gi=gi, dff=dff, chunks=chunks)
    return pl.pallas_call(
        kern,
        grid=(n_tiles,),
        in_specs=[
            pl.BlockSpec((tm, d), lambda i: (i, 0)),
            pl.BlockSpec((1, 1, N_MOD, d), lambda i: (layer, grp(i), 0, 0)),
            pl.BlockSpec((1, 3, d), lambda i: (layer, 0, 0)),
            _resident((1, d, 2 * dff), lambda i: (layer, 0, 0)),
            _resident((1, dff, d), lambda i: (layer, 0, 0)),
        ],
        out_specs=pl.BlockSpec((tm, d), lambda i: (i, 0)),
        out_shape=jax.ShapeDtypeStruct((n_tiles * tm, d), F32),
        scratch_shapes=[pltpu.VMEM((tm, dff), BF16)],
        compiler_params=_params(("arbitrary",)),
        name="ffn",
    )(xs, mod4, norm_g, wup, wdn)


_T_CQ = (0, 384)
_T_CKV = (384, 640)
_T_DQ = (640, 1152)
_T_DV = (1152, 1664)
_N_CKV = (0, 256)
_N_KR = (256, 384)
_N_KR_SW = (384, 512)
_N_DK = (512, 1024)
_N_DK_SW = (1024, 1536)
_N_CC = (1536, 2048)
_N_CX = (2048, 2560)


INPROJ_SPLIT = 2


def _tile_lanes(g, n):
    return jnp.concatenate([g] * (n // LANES), axis=1)


def _rms_t(x, g, n):
    ss = jnp.sum(x * x, axis=0, keepdims=True)
    return x * lax.rsqrt(ss * (1.0 / n) + EPS) * g


def _swap_row_blocks(x, width):
    parts = []
    for r0 in range(0, x.shape[0], 2 * width):
        parts += [x[r0 + width:r0 + 2 * width], x[r0:r0 + width]]
    return jnp.concatenate(parts, axis=0)


def _inproj_kernel(x_ref, mod_ref, ng_ref, wt_ref, wn_ref, gcqt_ref, wuqt_ref, gckvt_ref, wvt_ref,
                   gckv_ref, wk_ref, gqmt_ref, gkm_ref, gqdt_ref, gkd_ref,
                   cm_ref, sgm_ref, cd_ref, sgd_ref, cmt_ref, sgmt_ref, cdt_ref, sgdt_ref,
                   qmt_ref, km_ref, vmt_ref, dqt_ref, dk_ref, dvt_ref, u_ref):
    md = mod_ref[0, 0]
    tn = x_ref.shape[0] // INPROJ_SPLIT
    for t0 in range(0, x_ref.shape[0], tn):
        ts = slice(t0, t0 + tn)
        hf = _modulate(x_ref[ts, :], ng_ref[0, 1:2], md[3:4], md[4:5])
        h = hf.astype(BF16)
        ht = hf.T.astype(BF16)

        pt = _dot(wt_ref[0], ht)
        pn = _dot(h, wn_ref[0])

        def rows(seg):
            return pt[seg[0]:seg[1]]

        def cols(seg):
            return pn[:, seg[0]:seg[1]]

        cqn_t = _rms_t(rows(_T_CQ), _tile_lanes(gcqt_ref[0], tn), _T_CQ[1] - _T_CQ[0]).astype(BF16)
        q_raw_t = _dot(wuqt_ref[0], cqn_t)
        ckvn_t = _rms_t(rows(_T_CKV), _tile_lanes(gckvt_ref[0], tn), _T_CKV[1] - _T_CKV[0]).astype(BF16)
        vmt_ref[:, ts] = _dot(wvt_ref[0], ckvn_t).astype(BF16)
        dvt_ref[:, ts] = rows(_T_DV).astype(BF16)

        cmt, sgmt = cmt_ref[:, ts], sgmt_ref[:, ts]
        gqmt = _tile_lanes(gqmt_ref[0], tn)
        q_scale = MLA_QK ** -0.5 * LOG2E
        for hd in range(MLA_HEADS):
            blk = slice(hd * LANES, (hd + 1) * LANES)
            qn = _rms_t(q_raw_t[blk], gqmt, MLA_QK)
            qr = qn * cmt + _swap_row_blocks(qn, MLA_ROPE // 4) * sgmt
            qmt_ref[blk, ts] = (qr * q_scale).astype(BF16)

        cdt, sgdt = cdt_ref[:, ts], sgdt_ref[:, ts]
        gqdt = _tile_lanes(gqdt_ref[0], tn)
        d_scale = DIFF_DIM ** -0.5 * LOG2E
        dq_t = rows(_T_DQ)
        for hd in range(DIFF_HEADS):
            halves = []
            for comp in range(2):
                r0 = hd * LANES + comp * DIFF_DIM
                xc = dq_t[r0:r0 + DIFF_DIM]
                ss = jnp.sum(xc * xc, axis=0, keepdims=True)
                halves.append(xc * lax.rsqrt(ss * (1.0 / DIFF_DIM) + EPS))
            xn = jnp.concatenate(halves, axis=0) * gqdt
            xr = xn * cdt + _swap_row_blocks(xn, DIFF_DIM // 4) * sgdt
            dqt_ref[hd * LANES:(hd + 1) * LANES, ts] = (xr * d_scale).astype(BF16)

        ckvn = _rms(cols(_N_CKV), gckv_ref[0], _N_CKV[1] - _N_CKV[0]).astype(BF16)
        k_raw = _dot(ckvn, wk_ref[0])
        krope = cols(_N_KR)
        gc_m = gkm_ref[0, 0:1] * cm_ref[ts, :]
        rot_m = cols(_N_KR_SW) * (gkm_ref[0, 1:2] * sgm_ref[ts, :])
        for hd in range(MLA_HEADS):
            blk = slice(hd * LANES, (hd + 1) * LANES)
            kf = k_raw[:, blk] + krope
            r = lax.rsqrt(jnp.sum(kf * kf, axis=-1, keepdims=True) * (1.0 / MLA_QK) + EPS)
            km_ref[ts, blk] = ((kf * gc_m + rot_m) * r).astype(BF16)

        gc_d = gkd_ref[0, 0:1] * cd_ref[ts, :]
        gs_d = gkd_ref[0, 1:2] * sgd_ref[ts, :]
        lo = lax.broadcasted_iota(jnp.int32, (tn, LANES), 1) < DIFF_DIM
        dk, dk_sw = cols(_N_DK), cols(_N_DK_SW)
        for hd in range(DIFF_HEADS):
            blk = slice(hd * LANES, (hd + 1) * LANES)
            xh = dk[:, blk]
            sq = xh * xh
            s_lo = jnp.sum(jnp.where(lo, sq, 0.0), axis=-1, keepdims=True)
            s_hi = jnp.sum(jnp.where(lo, 0.0, sq), axis=-1, keepdims=True)
            r = lax.rsqrt(jnp.where(lo, s_lo, s_hi) * (1.0 / DIFF_DIM) + EPS)
            dk_ref[ts, blk] = ((xh * gc_d + dk_sw[:, blk] * gs_d) * r).astype(BF16)

        u_ref[ts, :] = cols(_N_CC) * cols(_N_CX)


def _inproj_call(xs, mod4, norm_g, wt, wn, gcqt, wuqt, gckvt, wvt, gckv, wk, gqmt, gkm, gqdt, gkd, tabs, tabs_t,
                 *, layer, n_tiles, tm, grp, tab_blk):
    r, d = xs.shape

    def vec(n):
        return pl.BlockSpec((1, 1, n), lambda i: (layer, 0, 0))

    def res(a):
        return _resident((1,) + a.shape[1:], lambda i: (layer, 0, 0))

    def tab():
        return pl.BlockSpec((tm, LANES), lambda i: (tab_blk(i), 0))

    def tab_t():
        return pl.BlockSpec((LANES, tm), lambda i: (0, tab_blk(i)))

    def rows(n):
        return pl.BlockSpec((tm, n), lambda i: (i, 0))

    hm, hd = MLA_HEADS * LANES, DIFF_HEADS * LANES
    out_shapes = [
        jax.ShapeDtypeStruct((hm, r), BF16),
        jax.ShapeDtypeStruct((r, hm), BF16),
        jax.ShapeDtypeStruct((BRANCH_WIDTH, r), BF16),
        jax.ShapeDtypeStruct((hd, r), BF16),
        jax.ShapeDtypeStruct((r, hd), BF16),
        jax.ShapeDtypeStruct((hd, r), BF16),
        jax.ShapeDtypeStruct((r, BRANCH_WIDTH), F32),
    ]
    return pl.pallas_call(
        _inproj_kernel,
        grid=(n_tiles,),
        in_specs=[
            rows(d),
            pl.BlockSpec((1, 1, N_MOD, d), lambda i: (layer, grp(i), 0, 0)),
            pl.BlockSpec((1, 3, d), lambda i: (layer, 0, 0)),
            res(wt), res(wn), res(gcqt), res(wuqt), res(gckvt), res(wvt),
            vec(gckv.shape[2]), res(wk), res(gqmt), res(gkm), res(gqdt), res(gkd),
            tab(), tab(), tab(), tab(),
            tab_t(), tab_t(), tab_t(), tab_t(),
        ],
        out_specs=[rows(s.shape[1]) if s.shape[0] == r else pl.BlockSpec((s.shape[0], tm), lambda i: (0, i))
                   for s in out_shapes],
        out_shape=out_shapes,
        compiler_params=_params(("arbitrary",)),
        name="inproj",
    )(xs, mod4, norm_g, wt, wn, gcqt, wuqt, gckvt, wvt, gckv, wk, gqmt, gkm, gqdt, gkd, *tabs, *tabs_t)


KEY_CHUNK = 256
LOGIT_SLOTS = 3


def _col_stat(x, op):
    t, n = x.shape
    r = 64 if t % 64 == 0 else SUBLANES
    return op(op(x.reshape(t // r, r, n), axis=0), axis=0, keepdims=True)


def _flash_heads(n_heads, get_qt, kblk, vrows, k_refs, vt_refs, s_scr, finish):
    depth = s_scr.shape[0]
    chunks = [(k_ref, vt_ref, c0, min(KEY_CHUNK, k_ref.shape[0])) for k_ref, vt_ref in zip(k_refs, vt_refs)
              for c0 in range(0, k_ref.shape[0], min(KEY_CHUNK, k_ref.shape[0]))]
    steps = [(h, ci) for h in range(n_heads) for ci in range(len(chunks))]

    def logits(n):
        h, ci = steps[n]
        k_ref, _, c0, ck = chunks[ci]
        s_scr[n % depth, 0:ck] = _dot(k_ref[c0:c0 + ck, kblk(h)], get_qt(h))

    for n in range(min(depth - 1, len(steps))):
        logits(n)
    m = acc = None
    for n, (h, ci) in enumerate(steps):
        if n + depth - 1 < len(steps):
            logits(n + depth - 1)
        _, vt_ref, c0, ck = chunks[ci]
        s = s_scr[n % depth, 0:ck]
        cm = _col_stat(s, jnp.max)
        m_new = cm if ci == 0 else jnp.maximum(m, cm)
        p = jnp.exp2(s - m_new).astype(BF16)
        vt1 = jnp.concatenate([vt_ref[vrows(h), c0:c0 + ck], jnp.ones((BF16_ROWS, ck), BF16)], axis=0)
        pv = _dot(vt1, p)
        acc = pv if ci == 0 else jnp.exp2(m - m_new) * acc + pv
        m = m_new
        if ci == len(chunks) - 1:
            dv = acc.shape[0] - BF16_ROWS
            finish(h, acc[:dv] * (1.0 / acc[dv:dv + 1]))


def _mla_attn_kernel(*refs, n_seg):
    qt_ref = refs[0]
    k_refs = refs[1:1 + n_seg]
    vt_refs = refs[1 + n_seg:1 + 2 * n_seg]
    o_ref, s_scr = refs[-2:]
    held = []

    def finish(h, ot):
        held.append(ot)
        if h % 2 == 1:
            pair_t = jnp.concatenate(held, axis=0)
            o_ref[:, (h // 2) * LANES:(h // 2 + 1) * LANES] = pair_t.T.astype(BF16)
            held.clear()

    _flash_heads(MLA_HEADS, lambda h: qt_ref[h * LANES:(h + 1) * LANES, :],
                 lambda h: slice(h * LANES, (h + 1) * LANES), lambda h: slice(h * MLA_V, (h + 1) * MLA_V),
                 k_refs, vt_refs, s_scr, finish)


def _diff_attn_kernel(*refs, n_seg, lam_init):
    qt_ref = refs[0]
    k_refs = refs[1:1 + n_seg]
    vt_refs = refs[1 + n_seg:1 + 2 * n_seg]
    lam_ref, gs_ref = refs[1 + 2 * n_seg:3 + 2 * n_seg]
    o_ref, s_scr = refs[-2:]
    bq = qt_ref.shape[1]
    lv = lam_ref[0]
    lam = (jnp.exp(jnp.sum(lv[0:1] * lv[1:2], axis=-1, keepdims=True))
           - jnp.exp(jnp.sum(lv[2:3] * lv[3:4], axis=-1, keepdims=True)) + lam_init)
    gs = gs_ref[0]
    lo = lax.broadcasted_iota(jnp.int32, (LANES, bq), 0) < DIFF_DIM
    blk = lambda h: slice(h * LANES, (h + 1) * LANES)

    def get_qt(h):
        qt = qt_ref[blk(h), :]
        zero = jnp.zeros_like(qt)
        return jnp.concatenate([jnp.where(lo, qt, zero), jnp.where(lo, zero, qt)], axis=1)

    def finish(h, oc):
        o = (oc[:, :bq] - lam * oc[:, bq:]).T
        o_ref[:, blk(h)] = (_rms(o, gs, DIFF_V) * (1.0 - lam_init)).astype(BF16)

    _flash_heads(DIFF_HEADS, get_qt, blk, blk, k_refs, vt_refs, s_scr, finish)


def _attn_call(kern, qt, k, vt, extra, *, width_qk, width_v, q_cols, batch, seq, ctx, r_lat, bq,
               latent, name):
    extra_specs = [pl.BlockSpec((1,) + a.shape[1:], functools.partial(lambda l, b, i: (l, 0, 0), lyr))
                   for a, lyr in extra]
    extra_args = [a for a, _ in extra]
    ctx_blk0 = r_lat // ctx
    if latent:
        nq = seq // bq
        grid = (batch, nq)
        q_spec = pl.BlockSpec((width_qk, bq), lambda b, i: (0, b * nq + i))
        k_specs = [pl.BlockSpec((ctx, width_qk), lambda b, i: (ctx_blk0 + b, 0)),
                   pl.BlockSpec((seq, width_qk), lambda b, i: (b, 0))]
        v_specs = [pl.BlockSpec((width_v, ctx), lambda b, i: (0, ctx_blk0 + b)),
                   pl.BlockSpec((width_v, seq), lambda b, i: (0, b))]
        o_spec = pl.BlockSpec((bq, width_v), lambda b, i: (b * nq + i, 0))
        k_args, v_args, o_rows = [k, k], [vt, vt], r_lat
    else:
        grid = (batch, 1)
        q_spec = pl.BlockSpec((width_qk, ctx), lambda b, i: (0, ctx_blk0 + b))
        k_specs = [pl.BlockSpec((ctx, width_qk), lambda b, i: (ctx_blk0 + b, 0))]
        v_specs = [pl.BlockSpec((width_v, ctx), lambda b, i: (0, ctx_blk0 + b))]
        o_spec = pl.BlockSpec((ctx, width_v), lambda b, i: (b, 0))
        k_args, v_args, o_rows = [k], [vt], batch * ctx
    return pl.pallas_call(
        functools.partial(kern, n_seg=len(k_args)),
        grid=grid,
        in_specs=[q_spec] + k_specs + v_specs + extra_specs,
        out_specs=o_spec,
        out_shape=jax.ShapeDtypeStruct((o_rows, width_v), BF16),
        scratch_shapes=[pltpu.VMEM((LOGIT_SLOTS, KEY_CHUNK, q_cols * q_spec.block_shape[1]), F32)],
        compiler_params=_params(("arbitrary", "arbitrary")),
        name=name,
    )(qt, *k_args, *v_args, *extra_args)


_B_GATE = 3 * 1024


def _merge_kernel(x_ref, mod_ref, ng_ref, wgc_ref, oml_ref, omc_ref, odl_ref, odc_ref, u_ref, up_ref, un_ref,
                  cw_ref, wbr_ref, wo_ref, o_ref, *, n_lat_tiles, seq, ctx):
    i = pl.program_id(0)
    is_lat = i < n_lat_tiles
    tm, d = x_ref.shape
    x = x_ref[...]
    md = mod_ref[0, 0]
    h = _modulate(x, ng_ref[0, 1:2], md[3:4], md[4:5]).astype(BF16)

    u = u_ref[...]
    row = lax.broadcasted_iota(jnp.int32, (tm, 1), 0)
    seq_len = jnp.where(is_lat, seq, ctx)
    pos = (i * tm + row) & (seq_len - 1)
    u_dn = jnp.where(row == 0, up_ref[SUBLANES - 1:SUBLANES, :], pltpu.roll(u, 1, 0))
    u_dn = jnp.where(pos == 0, 0.0, u_dn)
    u_up = jnp.where(row == tm - 1, un_ref[0:1, :], pltpu.roll(u, tm - 1, 0))
    u_up = jnp.where(pos == seq_len - 1, 0.0, u_up)
    cw = cw_ref[0]
    y = cw[0:1] * u_dn + cw[1:2] * u + cw[2:3] * u_up
    conv = (_dot(h, wgc_ref[0, :, _B_GATE:]) * y).astype(BF16)

    branches = (jnp.where(is_lat, oml_ref[...], omc_ref[...]), jnp.where(is_lat, odl_ref[...], odc_ref[...]), conv)
    m = None
    for n, yb in enumerate(branches):
        g = _sigmoid(_dot(h, wgc_ref[0, :, n * d:(n + 1) * d]))
        t = g * _dot(yb, wbr_ref[0, n])
        m = t if m is None else m + t
    o_ref[...] = x + md[5:6] * _dot(m.astype(BF16), wo_ref[0])


def _merge_call(xs, mod4, norm_g, wgc, om_lat, om_ctx, od_lat, od_ctx, u, conv_w, wbr, wo, *, layer, n_tiles,
                tm, grp, n_lat_tiles, seq, ctx):
    r, d = xs.shape
    hb = tm // SUBLANES
    last8 = u.shape[0] // SUBLANES - 1

    def rows(n):
        return pl.BlockSpec((tm, n), lambda i: (i, 0))

    lat_rows = pl.BlockSpec((tm, BRANCH_WIDTH), lambda i: (jnp.minimum(i, n_lat_tiles - 1), 0))
    ctx_rows = pl.BlockSpec((tm, BRANCH_WIDTH), lambda i: (jnp.maximum(i - n_lat_tiles, 0), 0))

    kern = functools.partial(_merge_kernel, n_lat_tiles=n_lat_tiles, seq=seq, ctx=ctx)
    return pl.pallas_call(
        kern,
        grid=(n_tiles,),
        in_specs=[
            rows(d),
            pl.BlockSpec((1, 1, N_MOD, d), lambda i: (layer, grp(i), 0, 0)),
            pl.BlockSpec((1, 3, d), lambda i: (layer, 0, 0)),
            _resident((1,) + wgc.shape[1:], lambda i: (layer, 0, 0)),
            lat_rows, ctx_rows, lat_rows, ctx_rows, rows(BRANCH_WIDTH),
            pl.BlockSpec((SUBLANES, BRANCH_WIDTH), lambda i: (jnp.maximum(i * hb - 1, 0), 0)),
            pl.BlockSpec((SUBLANES, BRANCH_WIDTH), lambda i: (jnp.minimum((i + 1) * hb, last8), 0)),
            pl.BlockSpec((1,) + conv_w.shape[1:], lambda i: (layer, 0, 0)),
            _resident((1,) + wbr.shape[1:], lambda i: (layer, 0, 0, 0)),
            _resident((1, d, d), lambda i: (layer, 0, 0)),
        ],
        out_specs=rows(d),
        out_shape=jax.ShapeDtypeStruct((n_tiles * tm, d), F32),
        compiler_params=_params(("arbitrary",)),
        name="merge",
    )(xs, mod4, norm_g, wgc, om_lat, om_ctx, od_lat, od_ctx, u, u, u, conv_w, wbr, wo)


def _rope_tables(seq, tm):
    t = jnp.arange(seq)
    row = (t // GRID_W).astype(F32)[:, None]
    col = (t % GRID_W).astype(F32)[:, None]

    lane = jnp.arange(LANES)

    def axis_tables(rot_dim, rel, live):
        q4 = rot_dim // 4
        inv = ROPE_THETA ** (-jnp.arange(q4, dtype=F32) / q4)
        ang = jnp.where((rel // (2 * q4)) == 0, row, col) * inv[rel % q4][None, :]
        first = ((rel // q4) % 2) == 0
        c = jnp.where(live, jnp.cos(ang), 1.0)
        s = jnp.where(live, jnp.where(first, -jnp.sin(ang), jnp.sin(ang)), 0.0)
        ident = [jnp.ones((tm, LANES), F32), jnp.zeros((tm, LANES), F32)]
        return [jnp.concatenate([a, b], axis=0) for a, b in zip((c, s), ident)]

    mla_live = (lane >= MLA_NOPE) & (lane < MLA_QK)
    mla = axis_tables(MLA_ROPE, jnp.where(mla_live, lane - MLA_NOPE, 0), mla_live)
    diff = axis_tables(DIFF_DIM, lane % DIFF_DIM, lane >= 0)
    return mla + diff


def _pad_last(a, n):
    return jnp.pad(a, [(0, 0)] * (a.ndim - 1) + [(0, n - a.shape[-1])])


def kernel(x, c, ctx, c_ctx, w_mod, b_mod, norm_g, ffn1_up, ffn1_down, ffn2_up, ffn2_down,
           w_in, g_cq, w_uq, g_ckv, w_ukv, g_q_mla, g_k_mla, g_q_diff, g_k_diff, lam,
           g_subln, conv_w, w_br, w_o):
    batch, seq, d = x.shape
    n_ctx = ctx.shape[1]
    depth = w_mod.shape[0]
    r_lat, r_ctx = batch * seq, batch * n_ctx
    tm = 512
    bq_mla, bq_diff = 512, 256
    assert seq % tm == 0 and r_ctx % tm == 0 and seq % bq_mla == 0 and r_lat % n_ctx == 0
    assert n_ctx % min(KEY_CHUNK, n_ctx) == 0 and seq % KEY_CHUNK == 0
    assert seq & (seq - 1) == 0 and n_ctx & (n_ctx - 1) == 0 and seq % GRID_W == 0
    assert d == 1024 and w_br.shape[2] == BRANCH_WIDTH
    n_lat_tiles, n_all_tiles = r_lat // tm, (r_lat + r_ctx) // tm
    tiles_per_batch = seq // tm

    def grp(i):
        return jnp.where(i < n_lat_tiles, i // tiles_per_batch, batch)

    def tab_blk(i):
        return jnp.where(i < n_lat_tiles, i % tiles_per_batch, tiles_per_batch)

    rank_q, rank_kv = g_cq.shape[1], g_ckv.shape[1]
    o = [0]
    for n in (rank_q, rank_kv, MLA_ROPE, 512, 512, 512, 512, 512, 512, 3 * d):
        o.append(o[-1] + n)
    seg = lambda k: w_in[:, :, o[k]:o[k + 1]]
    zeros = lambda n: jnp.zeros((depth, d, n), w_in.dtype)
    swap = lambda a: jnp.swapaxes(a, 1, 2)
    lanes = lambda g: jnp.broadcast_to(g[:, :, None], g.shape + (LANES,))
    wt = swap(jnp.concatenate([seg(0), seg(1), seg(3), seg(5)], axis=-1).astype(BF16))
    def partner(a, rot_dim):
        q4 = rot_dim // 4
        g = a.reshape(a.shape[:-1] + (a.shape[-1] // rot_dim, 2, 2, q4))
        return jnp.flip(g, axis=-2).reshape(a.shape)

    def place_rope(a):
        return jnp.concatenate([jnp.zeros(a.shape[:-1] + (MLA_NOPE,), a.dtype), a,
                                jnp.zeros(a.shape[:-1] + (LANES - MLA_QK,), a.dtype)], axis=-1)

    wn = jnp.concatenate([seg(1), place_rope(seg(2)), place_rope(partner(seg(2), MLA_ROPE)),
                          seg(4), partner(seg(4), DIFF_DIM), seg(7), seg(8)], axis=-1).astype(BF16)
    wgc = jnp.concatenate([seg(9), seg(6)], axis=-1).astype(BF16)
    wuq = _pad_last(w_uq.reshape(depth, rank_q, MLA_HEADS, MLA_QK), LANES)
    wuqt = swap(wuq.reshape(depth, rank_q, MLA_HEADS * LANES)).astype(BF16)
    wkv4 = w_ukv.reshape(depth, rank_kv, MLA_HEADS, MLA_NOPE + MLA_V)
    wk = _pad_last(wkv4[..., :MLA_NOPE], LANES).reshape(depth, rank_kv, MLA_HEADS * LANES).astype(BF16)
    wvt = swap(wkv4[..., MLA_NOPE:].reshape(depth, rank_kv, MLA_HEADS * MLA_V)).astype(BF16)
    ffn_w = [(ffn1_up.astype(BF16), ffn1_down.astype(BF16)), (ffn2_up.astype(BF16), ffn2_down.astype(BF16))]
    wbr = w_br.astype(BF16)
    wo = w_o.astype(BF16)
    gcqt = lanes(g_cq)
    gckvt = lanes(g_ckv)
    gckv = g_ckv[:, None, :]
    gqmt = lanes(_pad_last(g_q_mla, LANES))
    gkm = jnp.stack([_pad_last(g_k_mla, LANES),
                     place_rope(partner(g_k_mla[:, MLA_NOPE:], MLA_ROPE))], axis=1)
    gqdt = lanes(jnp.tile(g_q_diff, (1, 2)))
    gkd = jnp.stack([jnp.tile(g_k_diff, (1, 2)), jnp.tile(partner(g_k_diff, DIFF_DIM), (1, 2))], axis=1)
    gsub = g_subln[:, None, :]
    tabs = _rope_tables(seq, tm)
    tabs_t = [t.T for t in tabs]

    cond = jnp.concatenate([c, c_ctx[None, :]], axis=0)
    rows_pad = (batch + 1 + SUBLANES - 1) // SUBLANES * SUBLANES
    cond = jnp.pad(cond, ((0, rows_pad - batch - 1), (0, 0)))
    mod4 = _mod_call(cond, w_mod, b_mod).reshape(depth, rows_pad, N_MOD, d)

    xs = jnp.concatenate([x.reshape(r_lat, d), ctx.reshape(r_ctx, d)], axis=0)
    attn_kw = dict(batch=batch, seq=seq, ctx=n_ctx, r_lat=r_lat)
    for l in range(depth):
        last = l == depth - 1
        lam_init = 0.8 - 0.6 * math.exp(-0.3 * l)
        xs = _ffn_call(xs, mod4, norm_g, *ffn_w[0], layer=l, k0=0, gi=0,
                       n_tiles=n_all_tiles, tm=tm, grp=grp)
        qmt, km, vmt, dqt, dk, dvt, u = _inproj_call(
            xs, mod4, norm_g, wt, wn, gcqt, wuqt, gckvt, wvt, gckv, wk, gqmt, gkm, gqdt, gkd, tabs, tabs_t,
            layer=l, n_tiles=n_all_tiles, tm=tm, grp=grp, tab_blk=tab_blk)
        diff_kern = functools.partial(_diff_attn_kernel, lam_init=lam_init)
        diff_extra = [(lam, l), (gsub, l)]
        mla_kw = dict(width_qk=MLA_HEADS * LANES, width_v=BRANCH_WIDTH, q_cols=1, bq=bq_mla, **attn_kw)
        diff_kw = dict(width_qk=DIFF_HEADS * LANES, width_v=BRANCH_WIDTH, q_cols=2, bq=bq_diff, **attn_kw)
        om = _attn_call(_mla_attn_kernel, qmt, km, vmt, [], latent=True, name="mla_lat", **mla_kw)
        od = _attn_call(diff_kern, dqt, dk, dvt, diff_extra, latent=True, name="diff_lat", **diff_kw)
        n_tiles, om_ctx, od_ctx = n_lat_tiles, om, od
        if not last:
            om_ctx = _attn_call(_mla_attn_kernel, qmt, km, vmt, [], latent=False, name="mla_ctx", **mla_kw)
            od_ctx = _attn_call(diff_kern, dqt, dk, dvt, diff_extra, latent=False, name="diff_ctx", **diff_kw)
            n_tiles = n_all_tiles
        xs = _merge_call(xs, mod4, norm_g, wgc, om, om_ctx, od, od_ctx, u, conv_w, wbr, wo, layer=l,
                         n_tiles=n_tiles, tm=tm, grp=grp, n_lat_tiles=n_lat_tiles, seq=seq, ctx=n_ctx)
        xs = _ffn_call(xs, mod4, norm_g, *ffn_w[1], layer=l, k0=6, gi=2,
                       n_tiles=n_tiles, tm=tm, grp=grp)
    return xs.reshape(batch, seq, d)
```

```python
import functools
import math

import jax
import jax.numpy as jnp
from jax import lax
from jax.experimental import pallas as pl
from jax.experimental.pallas import tpu as pltpu

GRID_W = 64
N_MOD = 9
MLA_HEADS = 8
MLA_NOPE = 64
MLA_ROPE = 32
MLA_QK = MLA_NOPE + MLA_ROPE
MLA_V = 64
DIFF_HEADS = 4
DIFF_DIM = 64
DIFF_V = 2 * DIFF_DIM
BRANCH_WIDTH = 512
ROPE_THETA = 10000.0
EPS = 1e-6
LOG2E = math.log2(math.e)

LANES = 128
SUBLANES = 8
BF16_ROWS = 16
MXU_COLS = 256
FFN_CHUNK_ELEMS = 768 * 1024
VMEM_LIMIT_BYTES = 56 * 1024 * 1024
BF16 = jnp.bfloat16
F32 = jnp.float32


def _dot(a, b):
    return jnp.dot(a, b, preferred_element_type=F32)


def _dot_nt(a, b):
    return lax.dot_general(a, b, (((1,), (1,)), ((), ())), preferred_element_type=F32)


def _rms(x, g, n):
    ss = jnp.sum(x * x, axis=-1, keepdims=True)
    return x * lax.rsqrt(ss * (1.0 / n) + EPS) * g


def _modulate(x, g, shift, scale):
    return _rms(x, g, x.shape[-1]) * (1.0 + scale) + shift


def _sigmoid(x):
    return 1.0 / (1.0 + jnp.exp(-x))


def _params(sem):
    return pltpu.CompilerParams(dimension_semantics=sem, vmem_limit_bytes=VMEM_LIMIT_BYTES)


def _resident(block_shape, index_map):
    return pl.BlockSpec(block_shape, index_map, pipeline_mode=pl.Buffered(1))


def _mod_kernel(cond_ref, w_ref, b_ref, o_ref):
    c = cond_ref[...]
    a = (c * _sigmoid(c)).astype(BF16)
    o_ref[0] = _dot(a, w_ref[0].astype(BF16)) + b_ref[0]


def _mod_call(cond, w_mod, b_mod):
    depth, d, nd = w_mod.shape
    rows = cond.shape[0]
    tn = d
    return pl.pallas_call(
        _mod_kernel,
        grid=(depth, nd // tn),
        in_specs=[
            pl.BlockSpec((rows, d), lambda l, j: (0, 0)),
            pl.BlockSpec((1, d, tn), lambda l, j: (l, 0, j)),
            pl.BlockSpec((1, 1, tn), lambda l, j: (l, 0, j)),
        ],
        out_specs=pl.BlockSpec((1, rows, tn), lambda l, j: (l, 0, j)),
        out_shape=jax.ShapeDtypeStruct((depth, rows, nd), F32),
        compiler_params=_params(("arbitrary", "arbitrary")),
        name="mod",
    )(cond, w_mod, b_mod.reshape(depth, 1, nd))


def _ffn_kernel(x_ref, mod_ref, ng_ref, wup_ref, wdn_ref, o_ref, u_scr, *, k0, gi, dff, chunks):
    x = x_ref[...]
    md = mod_ref[0, 0]
    h = _modulate(x, ng_ref[0, gi:gi + 1], md[k0:k0 + 1], md[k0 + 1:k0 + 2]).astype(BF16)
    for lo, hi in chunks:
        a = _dot(h, wup_ref[0, :, lo:hi])
        b = _dot(h, wup_ref[0, :, dff + lo:dff + hi])
        u_scr[:, lo:hi] = (a * _sigmoid(a) * b).astype(BF16)
    y = _dot(u_scr[...], wdn_ref[0])
    o_ref[...] = x + (0.5 * md[k0 + 2:k0 + 3]) * y


def _ffn_call(xs, mod4, norm_g, wup, wdn, *, layer, k0, gi, n_tiles, tm, grp):
    d = xs.shape[1]
    dff = wdn.shape[1]
    width = max(MXU_COLS, FFN_CHUNK_ELEMS // tm // MXU_COLS * MXU_COLS)
    chunks = tuple((lo, min(lo + width, dff)) for lo in range(0, dff, width))
    kern = functools.partial(_ffn_kernel, k0=k0, gi=gi, dff=dff, chunks=chunks)
    return pl.pallas_call(
        kern,
        grid=(n_tiles,),
        in_specs=[
            pl.BlockSpec((tm, d), lambda i: (i, 0)),
            pl.BlockSpec((1, 1, N_MOD, d), lambda i: (layer, grp(i), 0, 0)),
            pl.BlockSpec((1, 3, d), lambda i: (layer, 0, 0)),
            _resident((1, d, 2 * dff), lambda i: (layer, 0, 0)),
            _resident((1, dff, d), lambda i: (layer, 0, 0)),
        ],
        out_specs=pl.BlockSpec((tm, d), lambda i: (i, 0)),
        out_shape=jax.ShapeDtypeStruct((n_tiles * tm, d), F32),
        scratch_shapes=[pltpu.VMEM((tm, dff), BF16)],
        compiler_params=_params(("arbitrary",)),
        name="ffn",
    )(xs, mod4, norm_g, wup, wdn)


_T_CQ = (0, 384)
_T_CKV = (384, 640)
_T_DQ = (640, 1152)
_T_DV = (1152, 1664)
_N_CKV = (0, 256)
_N_KR = (256, 384)
_N_KR_SW = (384, 512)
_N_DK = (512, 1024)
_N_DK_SW = (1024, 1536)
_N_CC = (1536, 2048)
_N_CX = (2048, 2560)


INPROJ_SPLIT = 2


def _tile_lanes(g, n):
    return jnp.concatenate([g] * (n // LANES), axis=1)


def _rms_t(x, g, n):
    ss = jnp.sum(x * x, axis=0, keepdims=True)
    return x * lax.rsqrt(ss * (1.0 / n) + EPS) * g


def _swap_row_blocks(x, width):
    parts = []
    for r0 in range(0, x.shape[0], 2 * width):
        parts += [x[r0 + width:r0 + 2 * width], x[r0:r0 + width]]
    return jnp.concatenate(parts, axis=0)


def _inproj_kernel(x_ref, mod_ref, ng_ref, wt_ref, wn_ref, gcqt_ref, wuqt_ref, gckvt_ref, wvt_ref,
                   gckv_ref, wk_ref, gqmt_ref, gkm_ref, gqdt_ref, gkd_ref,
                   cm_ref, sgm_ref, cd_ref, sgd_ref, cmt_ref, sgmt_ref, cdt_ref, sgdt_ref,
                   qmt_ref, km_ref, vmt_ref, dqt_ref, dk_ref, dvt_ref, u_ref, wt_scr):
    @pl.when(pl.program_id(0) == 0)
    def _():
        wt_scr[...] = wt_ref[0].T

    md = mod_ref[0, 0]
    tn = x_ref.shape[0] // INPROJ_SPLIT
    for t0 in range(0, x_ref.shape[0], tn):
        ts = slice(t0, t0 + tn)
        hf = _modulate(x_ref[ts, :], ng_ref[0, 1:2], md[3:4], md[4:5])
        h = hf.astype(BF16)
        ht = hf.T.astype(BF16)

        pt = _dot(wt_scr[...], ht)
        pn = _dot(h, wn_ref[0])

        def rows(seg):
            return pt[seg[0]:seg[1]]

        def cols(seg):
            return pn[:, seg[0]:seg[1]]

        cqn_t = _rms_t(rows(_T_CQ), _tile_lanes(gcqt_ref[0], tn), _T_CQ[1] - _T_CQ[0]).astype(BF16)
        q_raw_t = _dot(wuqt_ref[0], cqn_t)
        ckvn_t = _rms_t(rows(_T_CKV), _tile_lanes(gckvt_ref[0], tn), _T_CKV[1] - _T_CKV[0]).astype(BF16)
        vmt_ref[:, ts] = _dot(wvt_ref[0], ckvn_t).astype(BF16)
        dvt_ref[:, ts] = rows(_T_DV).astype(BF16)

        cmt, sgmt = cmt_ref[:, ts], sgmt_ref[:, ts]
        gqmt = _tile_lanes(gqmt_ref[0], tn)
        q_scale = MLA_QK ** -0.5 * LOG2E
        for hd in range(MLA_HEADS):
            blk = slice(hd * LANES, (hd + 1) * LANES)
            qn = _rms_t(q_raw_t[blk], gqmt, MLA_QK)
            qr = qn * cmt + _swap_row_blocks(qn, MLA_ROPE // 4) * sgmt
            qmt_ref[blk, ts] = (qr * q_scale).astype(BF16)

        cdt, sgdt = cdt_ref[:, ts], sgdt_ref[:, ts]
        gqdt = _tile_lanes(gqdt_ref[0], tn)
        d_scale = DIFF_DIM ** -0.5 * LOG2E
        dq_t = rows(_T_DQ)
        for hd in range(DIFF_HEADS):
            halves = []
            for comp in range(2):
                r0 = hd * LANES + comp * DIFF_DIM
                xc = dq_t[r0:r0 + DIFF_DIM]
                ss = jnp.sum(xc * xc, axis=0, keepdims=True)
                halves.append(xc * lax.rsqrt(ss * (1.0 / DIFF_DIM) + EPS))
            xn = jnp.concatenate(halves, axis=0) * gqdt
            xr = xn * cdt + _swap_row_blocks(xn, DIFF_DIM // 4) * sgdt
            dqt_ref[hd * LANES:(hd + 1) * LANES, ts] = (xr * d_scale).astype(BF16)

        ckvn = _rms(cols(_N_CKV), gckv_ref[0], _N_CKV[1] - _N_CKV[0]).astype(BF16)
        k_raw = _dot(ckvn, wk_ref[0])
        krope = cols(_N_KR)
        gc_m = gkm_ref[0, 0:1] * cm_ref[ts, :]
        rot_m = cols(_N_KR_SW) * (gkm_ref[0, 1:2] * sgm_ref[ts, :])
        for hd in range(MLA_HEADS):
            blk = slice(hd * LANES, (hd + 1) * LANES)
            kf = k_raw[:, blk] + krope
            r = lax.rsqrt(jnp.sum(kf * kf, axis=-1, keepdims=True) * (1.0 / MLA_QK) + EPS)
            km_ref[ts, blk] = ((kf * gc_m + rot_m) * r).astype(BF16)

        gc_d = gkd_ref[0, 0:1] * cd_ref[ts, :]
        gs_d = gkd_ref[0, 1:2] * sgd_ref[ts, :]
        lo = lax.broadcasted_iota(jnp.int32, (tn, LANES), 1) < DIFF_DIM
        dk, dk_sw = cols(_N_DK), cols(_N_DK_SW)
        for hd in range(DIFF_HEADS):
            blk = slice(hd * LANES, (hd + 1) * LANES)
            xh = dk[:, blk]
            sq = xh * xh
            s_lo = jnp.sum(jnp.where(lo, sq, 0.0), axis=-1, keepdims=True)
            s_hi = jnp.sum(jnp.where(lo, 0.0, sq), axis=-1, keepdims=True)
            r = lax.rsqrt(jnp.where(lo, s_lo, s_hi) * (1.0 / DIFF_DIM) + EPS)
            dk_ref[ts, blk] = ((xh * gc_d + dk_sw[:, blk] * gs_d) * r).astype(BF16)

        u_ref[ts, :] = cols(_N_CC) * cols(_N_CX)


def _inproj_call(xs, mod4, norm_g, wt, wn, gcqt, wuqt, gckvt, wvt, gckv, wk, gqmt, gkm, gqdt, gkd, tabs, tabs_t,
                 *, layer, n_tiles, tm, grp, tab_blk):
    r, d = xs.shape

    def vec(n):
        return pl.BlockSpec((1, 1, n), lambda i: (layer, 0, 0))

    def res(a):
        return _resident((1,) + a.shape[1:], lambda i: (layer, 0, 0))

    def tab():
        return pl.BlockSpec((tm, LANES), lambda i: (tab_blk(i), 0))

    def tab_t():
        return pl.BlockSpec((LANES, tm), lambda i: (0, tab_blk(i)))

    def rows(n):
        return pl.BlockSpec((tm, n), lambda i: (i, 0))

    hm, hd = MLA_HEADS * LANES, DIFF_HEADS * LANES
    out_shapes = [
        jax.ShapeDtypeStruct((hm, r), BF16),
        jax.ShapeDtypeStruct((r, hm), BF16),
        jax.ShapeDtypeStruct((BRANCH_WIDTH, r), BF16),
        jax.ShapeDtypeStruct((hd, r), BF16),
        jax.ShapeDtypeStruct((r, hd), BF16),
        jax.ShapeDtypeStruct((hd, r), BF16),
        jax.ShapeDtypeStruct((r, BRANCH_WIDTH), F32),
    ]
    return pl.pallas_call(
        _inproj_kernel,
        grid=(n_tiles,),
        in_specs=[
            rows(d),
            pl.BlockSpec((1, 1, N_MOD, d), lambda i: (layer, grp(i), 0, 0)),
            pl.BlockSpec((1, 3, d), lambda i: (layer, 0, 0)),
            res(wt), res(wn), res(gcqt), res(wuqt), res(gckvt), res(wvt),
            vec(gckv.shape[2]), res(wk), res(gqmt), res(gkm), res(gqdt), res(gkd),
            tab(), tab(), tab(), tab(),
            tab_t(), tab_t(), tab_t(), tab_t(),
        ],
        out_specs=[rows(s.shape[1]) if s.shape[0] == r else pl.BlockSpec((s.shape[0], tm), lambda i: (0, i))
                   for s in out_shapes],
        out_shape=out_shapes,
        scratch_shapes=[pltpu.VMEM((wt.shape[2], wt.shape[1]), BF16)],
        compiler_params=_params(("arbitrary",)),
        name="inproj",
    )(xs, mod4, norm_g, wt, wn, gcqt, wuqt, gckvt, wvt, gckv, wk, gqmt, gkm, gqdt, gkd, *tabs, *tabs_t)


KEY_CHUNK = 256
LOGIT_SLOTS = 3


def _col_stat(x, op):
    t, n = x.shape
    r = 64 if t % 64 == 0 else SUBLANES
    return op(op(x.reshape(t // r, r, n), axis=0), axis=0, keepdims=True)


def _flash_heads(n_heads, get_qt, kblk, vrows, k_refs, vt_refs, s_scr, finish):
    depth = s_scr.shape[0]
    chunks = [(k_ref, vt_ref, c0, min(KEY_CHUNK, k_ref.shape[0])) for k_ref, vt_ref in zip(k_refs, vt_refs)
              for c0 in range(0, k_ref.shape[0], min(KEY_CHUNK, k_ref.shape[0]))]
    steps = [(h, ci) for h in range(n_heads) for ci in range(len(chunks))]

    def logits(n):
        h, ci = steps[n]
        k_ref, _, c0, ck = chunks[ci]
        s_scr[n % depth, 0:ck] = _dot(k_ref[c0:c0 + ck, kblk(h)], get_qt(h))

    for n in range(min(depth - 1, len(steps))):
        logits(n)
    m = acc = None
    for n, (h, ci) in enumerate(steps):
        if n + depth - 1 < len(steps):
            logits(n + depth - 1)
        _, vt_ref, c0, ck = chunks[ci]
        s = s_scr[n % depth, 0:ck]
        cm = _col_stat(s, jnp.max)
        m_new = cm if ci == 0 else jnp.maximum(m, cm)
        p = jnp.exp2(s - m_new).astype(BF16)
        vt1 = jnp.concatenate([vt_ref[vrows(h), c0:c0 + ck], jnp.ones((BF16_ROWS, ck), BF16)], axis=0)
        pv = _dot(vt1, p)
        acc = pv if ci == 0 else jnp.exp2(m - m_new) * acc + pv
        m = m_new
        if ci == len(chunks) - 1:
            dv = acc.shape[0] - BF16_ROWS
            finish(h, acc[:dv] * (1.0 / acc[dv:dv + 1]))


def _mla_attn_kernel(*refs, n_seg):
    qt_ref = refs[0]
    k_refs = refs[1:1 + n_seg]
    vt_refs = refs[1 + n_seg:1 + 2 * n_seg]
    o_ref, s_scr = refs[-2:]
    held = []

    def finish(h, ot):
        held.append(ot)
        if h % 2 == 1:
            pair_t = jnp.concatenate(held, axis=0)
            o_ref[:, (h // 2) * LANES:(h // 2 + 1) * LANES] = pair_t.T.astype(BF16)
            held.clear()

    _flash_heads(MLA_HEADS, lambda h: qt_ref[h * LANES:(h + 1) * LANES, :],
                 lambda h: slice(h * LANES, (h + 1) * LANES), lambda h: slice(h * MLA_V, (h + 1) * MLA_V),
                 k_refs, vt_refs, s_scr, finish)


def _diff_attn_kernel(*refs, n_seg, lam_init):
    qt_ref = refs[0]
    k_refs = refs[1:1 + n_seg]
    vt_refs = refs[1 + n_seg:1 + 2 * n_seg]
    lam_ref, gs_ref = refs[1 + 2 * n_seg:3 + 2 * n_seg]
    o_ref, s_scr = refs[-2:]
    bq = qt_ref.shape[1]
    lv = lam_ref[0]
    lam = (jnp.exp(jnp.sum(lv[0:1] * lv[1:2], axis=-1, keepdims=True))
           - jnp.exp(jnp.sum(lv[2:3] * lv[3:4], axis=-1, keepdims=True)) + lam_init)
    gs = gs_ref[0]
    lo = lax.broadcasted_iota(jnp.int32, (LANES, bq), 0) < DIFF_DIM
    blk = lambda h: slice(h * LANES, (h + 1) * LANES)

    def get_qt(h):
        qt = qt_ref[blk(h), :]
        zero = jnp.zeros_like(qt)
        return jnp.concatenate([jnp.where(lo, qt, zero), jnp.where(lo, zero, qt)], axis=1)

    def finish(h, oc):
        o = (oc[:, :bq] - lam * oc[:, bq:]).T
        o_ref[:, blk(h)] = (_rms(o, gs, DIFF_V) * (1.0 - lam_init)).astype(BF16)

    _flash_heads(DIFF_HEADS, get_qt, blk, blk, k_refs, vt_refs, s_scr, finish)


def _attn_call(kern, qt, k, vt, extra, *, width_qk, width_v, q_cols, batch, seq, ctx, r_lat, bq,
               latent, name):
    extra_specs = [pl.BlockSpec((1,) + a.shape[1:], functools.partial(lambda l, b, i: (l, 0, 0), lyr))
                   for a, lyr in extra]
    extra_args = [a for a, _ in extra]
    ctx_blk0 = r_lat // ctx
    if latent:
        nq = seq // bq
        grid = (batch, nq)
        q_spec = pl.BlockSpec((width_qk, bq), lambda b, i: (0, b * nq + i))
        k_specs = [pl.BlockSpec((ctx, width_qk), lambda b, i: (ctx_blk0 + b, 0)),
                   pl.BlockSpec((seq, width_qk), lambda b, i: (b, 0))]
        v_specs = [pl.BlockSpec((width_v, ctx), lambda b, i: (0, ctx_blk0 + b)),
                   pl.BlockSpec((width_v, seq), lambda b, i: (0, b))]
        o_spec = pl.BlockSpec((bq, width_v), lambda b, i: (b * nq + i, 0))
        k_args, v_args, o_rows = [k, k], [vt, vt], r_lat
    else:
        grid = (batch, 1)
        q_spec = pl.BlockSpec((width_qk, ctx), lambda b, i: (0, ctx_blk0 + b))
        k_specs = [pl.BlockSpec((ctx, width_qk), lambda b, i: (ctx_blk0 + b, 0))]
        v_specs = [pl.BlockSpec((width_v, ctx), lambda b, i: (0, ctx_blk0 + b))]
        o_spec = pl.BlockSpec((ctx, width_v), lambda b, i: (b, 0))
        k_args, v_args, o_rows = [k], [vt], batch * ctx
    return pl.pallas_call(
        functools.partial(kern, n_seg=len(k_args)),
        grid=grid,
        in_specs=[q_spec] + k_specs + v_specs + extra_specs,
        out_specs=o_spec,
        out_shape=jax.ShapeDtypeStruct((o_rows, width_v), BF16),
        scratch_shapes=[pltpu.VMEM((LOGIT_SLOTS, KEY_CHUNK, q_cols * q_spec.block_shape[1]), F32)],
        compiler_params=_params(("arbitrary", "arbitrary")),
        name=name,
    )(qt, *k_args, *v_args, *extra_args)


_B_GATE = 3 * 1024


def _merge_kernel(x_ref, mod_ref, ng_ref, wgc_ref, oml_ref, omc_ref, odl_ref, odc_ref, u_ref, up_ref, un_ref,
                  cw_ref, wbr_ref, wo_ref, o_ref, *, n_lat_tiles, seq, ctx):
    i = pl.program_id(0)
    is_lat = i < n_lat_tiles
    tm, d = x_ref.shape
    x = x_ref[...]
    md = mod_ref[0, 0]
    h = _modulate(x, ng_ref[0, 1:2], md[3:4], md[4:5]).astype(BF16)

    u = u_ref[...]
    row = lax.broadcasted_iota(jnp.int32, (tm, 1), 0)
    seq_len = jnp.where(is_lat, seq, ctx)
    pos = (i * tm + row) & (seq_len - 1)
    u_dn = jnp.where(row == 0, up_ref[SUBLANES - 1:SUBLANES, :], pltpu.roll(u, 1, 0))
    u_dn = jnp.where(pos == 0, 0.0, u_dn)
    u_up = jnp.where(row == tm - 1, un_ref[0:1, :], pltpu.roll(u, tm - 1, 0))
    u_up = jnp.where(pos == seq_len - 1, 0.0, u_up)
    cw = cw_ref[0]
    y = cw[0:1] * u_dn + cw[1:2] * u + cw[2:3] * u_up
    conv = (_dot(h, wgc_ref[0, :, _B_GATE:]) * y).astype(BF16)

    branches = (jnp.where(is_lat, oml_ref[...], omc_ref[...]), jnp.where(is_lat, odl_ref[...], odc_ref[...]), conv)
    m = None
    for n, yb in enumerate(branches):
        g = _sigmoid(_dot(h, wgc_ref[0, :, n * d:(n + 1) * d]))
        t = g * _dot(yb, wbr_ref[0, n])
        m = t if m is None else m + t
    o_ref[...] = x + md[5:6] * _dot(m.astype(BF16), wo_ref[0])


def _merge_call(xs, mod4, norm_g, wgc, om_lat, om_ctx, od_lat, od_ctx, u, conv_w, wbr, wo, *, layer, n_tiles,
                tm, grp, n_lat_tiles, seq, ctx):
    r, d = xs.shape
    hb = tm // SUBLANES
    last8 = u.shape[0] // SUBLANES - 1

    def rows(n):
        return pl.BlockSpec((tm, n), lambda i: (i, 0))

    lat_rows = pl.BlockSpec((tm, BRANCH_WIDTH), lambda i: (jnp.minimum(i, n_lat_tiles - 1), 0))
    ctx_rows = pl.BlockSpec((tm, BRANCH_WIDTH), lambda i: (jnp.maximum(i - n_lat_tiles, 0), 0))

    kern = functools.partial(_merge_kernel, n_lat_tiles=n_lat_tiles, seq=seq, ctx=ctx)
    return pl.pallas_call(
        kern,
        grid=(n_tiles,),
        in_specs=[
            rows(d),
            pl.BlockSpec((1, 1, N_MOD, d), lambda i: (layer, grp(i), 0, 0)),
            pl.BlockSpec((1, 3, d), lambda i: (layer, 0, 0)),
            _resident((1,) + wgc.shape[1:], lambda i: (layer, 0, 0)),
            lat_rows, ctx_rows, lat_rows, ctx_rows, rows(BRANCH_WIDTH),
            pl.BlockSpec((SUBLANES, BRANCH_WIDTH), lambda i: (jnp.maximum(i * hb - 1, 0), 0)),
            pl.BlockSpec((SUBLANES, BRANCH_WIDTH), lambda i: (jnp.minimum((i + 1) * hb, last8), 0)),
            pl.BlockSpec((1,) + conv_w.shape[1:], lambda i: (layer, 0, 0)),
            _resident((1,) + wbr.shape[1:], lambda i: (layer, 0, 0, 0)),
            _resident((1, d, d), lambda i: (layer, 0, 0)),
        ],
        out_specs=rows(d),
        out_shape=jax.ShapeDtypeStruct((n_tiles * tm, d), F32),
        compiler_params=_params(("arbitrary",)),
        name="merge",
    )(xs, mod4, norm_g, wgc, om_lat, om_ctx, od_lat, od_ctx, u, u, u, conv_w, wbr, wo)


def _rope_tables(seq, tm):
    t = jnp.arange(seq)
    row = (t // GRID_W).astype(F32)[:, None]
    col = (t % GRID_W).astype(F32)[:, None]

    lane = jnp.arange(LANES)

    def axis_tables(rot_dim, rel, live):
        q4 = rot_dim // 4
        inv = ROPE_THETA ** (-jnp.arange(q4, dtype=F32) / q4)
        ang = jnp.where((rel // (2 * q4)) == 0, row, col) * inv[rel % q4][None, :]
        first = ((rel // q4) % 2) == 0
        c = jnp.where(live, jnp.cos(ang), 1.0)
        s = jnp.where(live, jnp.where(first, -jnp.sin(ang), jnp.sin(ang)), 0.0)
        ident = [jnp.ones((tm, LANES), F32), jnp.zeros((tm, LANES), F32)]
        return [jnp.concatenate([a, b], axis=0) for a, b in zip((c, s), ident)]

    mla_live = (lane >= MLA_NOPE) & (lane < MLA_QK)
    mla = axis_tables(MLA_ROPE, jnp.where(mla_live, lane - MLA_NOPE, 0), mla_live)
    diff = axis_tables(DIFF_DIM, lane % DIFF_DIM, lane >= 0)
    return mla + diff


def _pad_last(a, n):
    return jnp.pad(a, [(0, 0)] * (a.ndim - 1) + [(0, n - a.shape[-1])])


def kernel(x, c, ctx, c_ctx, w_mod, b_mod, norm_g, ffn1_up, ffn1_down, ffn2_up, ffn2_down,
           w_in, g_cq, w_uq, g_ckv, w_ukv, g_q_mla, g_k_mla, g_q_diff, g_k_diff, lam,
           g_subln, conv_w, w_br, w_o):
    batch, seq, d = x.shape
    n_ctx = ctx.shape[1]
    depth = w_mod.shape[0]
    r_lat, r_ctx = batch * seq, batch * n_ctx
    tm, tm_ffn = 512, 1024
    bq_mla, bq_diff = 512, 256
    assert seq % tm == 0 and r_ctx % tm == 0 and seq % bq_mla == 0 and r_lat % n_ctx == 0
    assert seq % tm_ffn == 0 and r_ctx % tm_ffn == 0
    assert n_ctx % min(KEY_CHUNK, n_ctx) == 0 and seq % KEY_CHUNK == 0
    assert seq & (seq - 1) == 0 and n_ctx & (n_ctx - 1) == 0 and seq % GRID_W == 0
    assert d == 1024 and w_br.shape[2] == BRANCH_WIDTH
    n_lat_tiles, n_all_tiles = r_lat // tm, (r_lat + r_ctx) // tm
    tiles_per_batch = seq // tm

    def grp_of(tile):
        return lambda i: jnp.where(i < r_lat // tile, i // (seq // tile), batch)

    grp, grp_ffn = grp_of(tm), grp_of(tm_ffn)

    def tab_blk(i):
        return jnp.where(i < n_lat_tiles, i % tiles_per_batch, tiles_per_batch)

    rank_q, rank_kv = g_cq.shape[1], g_ckv.shape[1]
    o = [0]
    for n in (rank_q, rank_kv, MLA_ROPE, 512, 512, 512, 512, 512, 512, 3 * d):
        o.append(o[-1] + n)
    seg = lambda k: w_in[:, :, o[k]:o[k + 1]]
    zeros = lambda n: jnp.zeros((depth, d, n), w_in.dtype)
    swap = lambda a: jnp.swapaxes(a, 1, 2)
    lanes = lambda g: jnp.broadcast_to(g[:, :, None], g.shape + (LANES,))
    wt = jnp.concatenate([seg(0), seg(1), seg(3), seg(5)], axis=-1).astype(BF16)
    def partner(a, rot_dim):
        q4 = rot_dim // 4
        g = a.reshape(a.shape[:-1] + (a.shape[-1] // rot_dim, 2, 2, q4))
        return jnp.flip(g, axis=-2).reshape(a.shape)

    def place_rope(a):
        return jnp.concatenate([jnp.zeros(a.shape[:-1] + (MLA_NOPE,), a.dtype), a,
                                jnp.zeros(a.shape[:-1] + (LANES - MLA_QK,), a.dtype)], axis=-1)

    wn = jnp.concatenate([seg(1), place_rope(seg(2)), place_rope(partner(seg(2), MLA_ROPE)),
                          seg(4), partner(seg(4), DIFF_DIM), seg(7), seg(8)], axis=-1).astype(BF16)
    wgc = jnp.concatenate([seg(9), seg(6)], axis=-1).astype(BF16)
    wuq = _pad_last(w_uq.reshape(depth, rank_q, MLA_HEADS, MLA_QK), LANES)
    wuqt = swap(wuq.reshape(depth, rank_q, MLA_HEADS * LANES)).astype(BF16)
    wkv4 = w_ukv.reshape(depth, rank_kv, MLA_HEADS, MLA_NOPE + MLA_V)
    wk = _pad_last(wkv4[..., :MLA_NOPE], LANES).reshape(depth, rank_kv, MLA_HEADS * LANES).astype(BF16)
    wvt = swap(wkv4[..., MLA_NOPE:].reshape(depth, rank_kv, MLA_HEADS * MLA_V)).astype(BF16)
    ffn_w = [(ffn1_up.astype(BF16), ffn1_down.astype(BF16)), (ffn2_up.astype(BF16), ffn2_down.astype(BF16))]
    wbr = w_br.astype(BF16)
    wo = w_o.astype(BF16)
    gcqt = lanes(g_cq)
    gckvt = lanes(g_ckv)
    gckv = g_ckv[:, None, :]
    gqmt = lanes(_pad_last(g_q_mla, LANES))
    gkm = jnp.stack([_pad_last(g_k_mla, LANES),
                     place_rope(partner(g_k_mla[:, MLA_NOPE:], MLA_ROPE))], axis=1)
    gqdt = lanes(jnp.tile(g_q_diff, (1, 2)))
    gkd = jnp.stack([jnp.tile(g_k_diff, (1, 2)), jnp.tile(partner(g_k_diff, DIFF_DIM), (1, 2))], axis=1)
    gsub = g_subln[:, None, :]
    tabs = _rope_tables(seq, tm)
    tabs_t = [t.T for t in tabs]

    cond = jnp.concatenate([c, c_ctx[None, :]], axis=0)
    rows_pad = (batch + 1 + SUBLANES - 1) // SUBLANES * SUBLANES
    cond = jnp.pad(cond, ((0, rows_pad - batch - 1), (0, 0)))
    mod4 = _mod_call(cond, w_mod, b_mod).reshape(depth, rows_pad, N_MOD, d)

    xs = jnp.concatenate([x.reshape(r_lat, d), ctx.reshape(r_ctx, d)], axis=0)
    attn_kw = dict(batch=batch, seq=seq, ctx=n_ctx, r_lat=r_lat)
    for l in range(depth):
        last = l == depth - 1
        lam_init = 0.8 - 0.6 * math.exp(-0.3 * l)
        xs = _ffn_call(xs, mod4, norm_g, *ffn_w[0], layer=l, k0=0, gi=0,
                       n_tiles=n_all_tiles * tm // tm_ffn, tm=tm_ffn, grp=grp_ffn)
        qmt, km, vmt, dqt, dk, dvt, u = _inproj_call(
            xs, mod4, norm_g, wt, wn, gcqt, wuqt, gckvt, wvt, gckv, wk, gqmt, gkm, gqdt, gkd, tabs, tabs_t,
            layer=l, n_tiles=n_all_tiles, tm=tm, grp=grp, tab_blk=tab_blk)
        diff_kern = functools.partial(_diff_attn_kernel, lam_init=lam_init)
        diff_extra = [(lam, l), (gsub, l)]
        mla_kw = dict(width_qk=MLA_HEADS * LANES, width_v=BRANCH_WIDTH, q_cols=1, bq=bq_mla, **attn_kw)
        diff_kw = dict(width_qk=DIFF_HEADS * LANES, width_v=BRANCH_WIDTH, q_cols=2, bq=bq_diff, **attn_kw)
        om = _attn_call(_mla_attn_kernel, qmt, km, vmt, [], latent=True, name="mla_lat", **mla_kw)
        od = _attn_call(diff_kern, dqt, dk, dvt, diff_extra, latent=True, name="diff_lat", **diff_kw)
        n_tiles, om_ctx, od_ctx = n_lat_tiles, om, od
        if not last:
            om_ctx = _attn_call(_mla_attn_kernel, qmt, km, vmt, [], latent=False, name="mla_ctx", **mla_kw)
            od_ctx = _attn_call(diff_kern, dqt, dk, dvt, diff_extra, latent=False, name="diff_ctx", **diff_kw)
            n_tiles = n_all_tiles
        xs = _merge_call(xs, mod4, norm_g, wgc, om, om_ctx, od, od_ctx, u, conv_w, wbr, wo, layer=l,
                         n_tiles=n_tiles, tm=tm, grp=grp, n_lat_tiles=n_lat_tiles, seq=seq, ctx=n_ctx)
        xs = _ffn_call(xs, mod4, norm_g, *ffn_w[1], layer=l, k0=6, gi=2,
                       n_tiles=n_tiles * tm // tm_ffn, tm=tm_ffn, grp=grp_ffn)
    return xs.reshape(batch, seq, d)
```

```python
import functools
import math

import jax
import jax.numpy as jnp
from jax import lax
from jax.experimental import pallas as pl
from jax.experimental.pallas import tpu as pltpu

GRID_W = 64
N_MOD = 9
MLA_HEADS = 8
MLA_NOPE = 64
MLA_ROPE = 32
MLA_QK = MLA_NOPE + MLA_ROPE
MLA_V = 64
DIFF_HEADS = 4
DIFF_DIM = 64
DIFF_V = 2 * DIFF_DIM
BRANCH_WIDTH = 512
ROPE_THETA = 10000.0
EPS = 1e-6
LOG2E = math.log2(math.e)

LANES = 128
SUBLANES = 8
BF16_ROWS = 16
MXU_COLS = 256
FFN_CHUNK_ELEMS = 768 * 1024
FFN_SPLIT = 2
VMEM_LIMIT_BYTES = 56 * 1024 * 1024
BF16 = jnp.bfloat16
F32 = jnp.float32


def _dot(a, b):
    return jnp.dot(a, b, preferred_element_type=F32)


def _rms(x, g, n):
    ss = jnp.sum(x * x, axis=-1, keepdims=True)
    return x * lax.rsqrt(ss * (1.0 / n) + EPS) * g


def _modulate(x, g, shift, scale):
    return _rms(x, g, x.shape[-1]) * (1.0 + scale) + shift


def _sigmoid(x):
    return 1.0 / (1.0 + jnp.exp(-x))


def _params(sem):
    return pltpu.CompilerParams(dimension_semantics=sem, vmem_limit_bytes=VMEM_LIMIT_BYTES)


def _resident(block_shape, index_map):
    return pl.BlockSpec(block_shape, index_map, pipeline_mode=pl.Buffered(1))


def _mod_kernel(cond_ref, w_ref, b_ref, o_ref):
    c = cond_ref[...]
    a = (c * _sigmoid(c)).astype(BF16)
    o_ref[0] = _dot(a, w_ref[0].astype(BF16)) + b_ref[0]


def _mod_call(cond, w_mod, b_mod):
    depth, d, nd = w_mod.shape
    rows = cond.shape[0]
    tn = d
    return pl.pallas_call(
        _mod_kernel,
        grid=(depth, nd // tn),
        in_specs=[
            pl.BlockSpec((rows, d), lambda l, j: (0, 0)),
            pl.BlockSpec((1, d, tn), lambda l, j: (l, 0, j)),
            pl.BlockSpec((1, 1, tn), lambda l, j: (l, 0, j)),
        ],
        out_specs=pl.BlockSpec((1, rows, tn), lambda l, j: (l, 0, j)),
        out_shape=jax.ShapeDtypeStruct((depth, rows, nd), F32),
        compiler_params=_params(("arbitrary", "arbitrary")),
        name="mod",
    )(cond, w_mod, b_mod.reshape(depth, 1, nd))


def _ffn_kernel(*refs, k0, gi, dff, chunks, n_first):
    x_refs, (mod_ref, ng_ref, wup_ref, wdn_ref, o_ref, u_scr) = refs[:-6], refs[-6:]
    md = mod_ref[0, 0]
    tn = o_ref.shape[0] // FFN_SPLIT
    for t0 in range(0, o_ref.shape[0], tn):
        ts = slice(t0, t0 + tn)
        x = x_refs[0][ts, :]
        if n_first is not None:
            x = jnp.where(pl.program_id(0) < n_first, x, x_refs[1][ts, :])
        h = _modulate(x, ng_ref[0, gi:gi + 1], md[k0:k0 + 1], md[k0 + 1:k0 + 2]).astype(BF16)
        for lo, hi in chunks:
            a = _dot(h, wup_ref[0, :, lo:hi])
            b = _dot(h, wup_ref[0, :, dff + lo:dff + hi])
            u_scr[ts, lo:hi] = (a * _sigmoid(a) * b).astype(BF16)
        y = _dot(u_scr[ts, :], wdn_ref[0])
        o_ref[ts, :] = x + (0.5 * md[k0 + 2:k0 + 3]) * y


def _ffn_call(xs, mod4, norm_g, wup, wdn, *, layer, k0, gi, n_tiles, tm, grp):
    xs = xs if isinstance(xs, tuple) else (xs,)
    d = xs[0].shape[1]
    dff = wdn.shape[1]
    width = max(MXU_COLS, FFN_CHUNK_ELEMS // (tm // FFN_SPLIT) // MXU_COLS * MXU_COLS)
    chunks = tuple((lo, min(lo + width, dff)) for lo in range(0, dff, width))
    n_first = xs[0].shape[0] // tm if len(xs) == 2 else None
    kern = functools.partial(_ffn_kernel, k0=k0, gi=gi, dff=dff, chunks=chunks, n_first=n_first)
    if n_first is None:
        x_specs = [pl.BlockSpec((tm, d), lambda i: (i, 0))]
    else:
        x_specs = [pl.BlockSpec((tm, d), lambda i: (jnp.minimum(i, n_first - 1), 0)),
                   pl.BlockSpec((tm, d), lambda i: (jnp.maximum(i - n_first, 0), 0))]
    return pl.pallas_call(
        kern,
        grid=(n_tiles,),
        in_specs=x_specs + [
            pl.BlockSpec((1, 1, N_MOD, d), lambda i: (layer, grp(i), 0, 0)),
            pl.BlockSpec((1, 3, d), lambda i: (layer, 0, 0)),
            _resident((1, d, 2 * dff), lambda i: (layer, 0, 0)),
            _resident((1, dff, d), lambda i: (layer, 0, 0)),
        ],
        out_specs=pl.BlockSpec((tm, d), lambda i: (i, 0)),
        out_shape=jax.ShapeDtypeStruct((n_tiles * tm, d), F32),
        scratch_shapes=[pltpu.VMEM((tm, dff), BF16)],
        compiler_params=_params(("arbitrary",)),
        name="ffn",
    )(*xs, mod4, norm_g, wup, wdn)


_T_CQ = (0, 384)
_T_CKV = (384, 640)
_T_DQ = (640, 1152)
_T_DV = (1152, 1664)
_N_CKV = (0, 256)
_N_KR = (256, 384)
_N_KR_SW = (384, 512)
_N_DK = (512, 1024)
_N_DK_SW = (1024, 1536)
_N_CC = (1536, 2048)
_N_CX = (2048, 2560)


INPROJ_SPLIT = 2


def _tile_lanes(g, n):
    return jnp.concatenate([g] * (n // LANES), axis=1)


def _rms_t(x, g, n):
    ss = jnp.sum(x * x, axis=0, keepdims=True)
    return x * lax.rsqrt(ss * (1.0 / n) + EPS) * g


def _swap_row_blocks(x, width):
    parts = []
    for r0 in range(0, x.shape[0], 2 * width):
        parts += [x[r0 + width:r0 + 2 * width], x[r0:r0 + width]]
    return jnp.concatenate(parts, axis=0)


def _inproj_kernel(x_ref, mod_ref, ng_ref, wt_ref, wn_ref, gcqt_ref, wuqt_ref, gckvt_ref, wvt_ref,
                   gckv_ref, wk_ref, gqmt_ref, gkm_ref, gqdt_ref, gkd_ref,
                   cm_ref, sgm_ref, cd_ref, sgd_ref, cmt_ref, sgmt_ref, cdt_ref, sgdt_ref,
                   qmt_ref, km_ref, vmt_ref, dqt_ref, dk_ref, dvt_ref, u_ref, wt_scr):
    @pl.when(pl.program_id(0) == 0)
    def _():
        wt_scr[...] = wt_ref[0].T

    md = mod_ref[0, 0]
    tn = x_ref.shape[0] // INPROJ_SPLIT
    for t0 in range(0, x_ref.shape[0], tn):
        ts = slice(t0, t0 + tn)
        hf = _modulate(x_ref[ts, :], ng_ref[0, 1:2], md[3:4], md[4:5])
        h = hf.astype(BF16)
        ht = hf.T.astype(BF16)

        pt = _dot(wt_scr[...], ht)
        pn = _dot(h, wn_ref[0])

        def rows(seg):
            return pt[seg[0]:seg[1]]

        def cols(seg):
            return pn[:, seg[0]:seg[1]]

        cqn_t = _rms_t(rows(_T_CQ), _tile_lanes(gcqt_ref[0], tn), _T_CQ[1] - _T_CQ[0]).astype(BF16)
        q_raw_t = _dot(wuqt_ref[0], cqn_t)
        ckvn_t = _rms_t(rows(_T_CKV), _tile_lanes(gckvt_ref[0], tn), _T_CKV[1] - _T_CKV[0]).astype(BF16)
        vmt_ref[:, ts] = _dot(wvt_ref[0], ckvn_t).astype(BF16)
        dvt_ref[:, ts] = rows(_T_DV).astype(BF16)

        cmt, sgmt = cmt_ref[:, ts], sgmt_ref[:, ts]
        gqmt = _tile_lanes(gqmt_ref[0], tn)
        q_scale = MLA_QK ** -0.5 * LOG2E
        for hd in range(MLA_HEADS):
            blk = slice(hd * LANES, (hd + 1) * LANES)
            qn = _rms_t(q_raw_t[blk], gqmt, MLA_QK)
            qr = qn * cmt + _swap_row_blocks(qn, MLA_ROPE // 4) * sgmt
            qmt_ref[blk, ts] = (qr * q_scale).astype(BF16)

        cdt, sgdt = cdt_ref[:, ts], sgdt_ref[:, ts]
        gqdt = _tile_lanes(gqdt_ref[0], tn)
        d_scale = DIFF_DIM ** -0.5 * LOG2E
        dq_t = rows(_T_DQ)
        for hd in range(DIFF_HEADS):
            halves = []
            for comp in range(2):
                r0 = hd * LANES + comp * DIFF_DIM
                xc = dq_t[r0:r0 + DIFF_DIM]
                ss = jnp.sum(xc * xc, axis=0, keepdims=True)
                halves.append(xc * lax.rsqrt(ss * (1.0 / DIFF_DIM) + EPS))
            xn = jnp.concatenate(halves, axis=0) * gqdt
            xr = xn * cdt + _swap_row_blocks(xn, DIFF_DIM // 4) * sgdt
            dqt_ref[hd * LANES:(hd + 1) * LANES, ts] = (xr * d_scale).astype(BF16)

        ckvn = _rms(cols(_N_CKV), gckv_ref[0], _N_CKV[1] - _N_CKV[0]).astype(BF16)
        k_raw = _dot(ckvn, wk_ref[0])
        krope = cols(_N_KR)
        gc_m = gkm_ref[0, 0:1] * cm_ref[ts, :]
        rot_m = cols(_N_KR_SW) * (gkm_ref[0, 1:2] * sgm_ref[ts, :])
        for hd in range(MLA_HEADS):
            blk = slice(hd * LANES, (hd + 1) * LANES)
            kf = k_raw[:, blk] + krope
            r = lax.rsqrt(jnp.sum(kf * kf, axis=-1, keepdims=True) * (1.0 / MLA_QK) + EPS)
            km_ref[ts, blk] = ((kf * gc_m + rot_m) * r).astype(BF16)

        gc_d = gkd_ref[0, 0:1] * cd_ref[ts, :]
        gs_d = gkd_ref[0, 1:2] * sgd_ref[ts, :]
        lo = lax.broadcasted_iota(jnp.int32, (tn, LANES), 1) < DIFF_DIM
        dk, dk_sw = cols(_N_DK), cols(_N_DK_SW)
        for hd in range(DIFF_HEADS):
            blk = slice(hd * LANES, (hd + 1) * LANES)
            xh = dk[:, blk]
            sq = xh * xh
            s_lo = jnp.sum(jnp.where(lo, sq, 0.0), axis=-1, keepdims=True)
            s_hi = jnp.sum(jnp.where(lo, 0.0, sq), axis=-1, keepdims=True)
            r = lax.rsqrt(jnp.where(lo, s_lo, s_hi) * (1.0 / DIFF_DIM) + EPS)
            dk_ref[ts, blk] = ((xh * gc_d + dk_sw[:, blk] * gs_d) * r).astype(BF16)

        u_ref[ts, :] = cols(_N_CC) * cols(_N_CX)


def _inproj_call(xs, mod4, norm_g, wt, wn, gcqt, wuqt, gckvt, wvt, gckv, wk, gqmt, gkm, gqdt, gkd, tabs, tabs_t,
                 *, layer, n_tiles, tm, grp, tab_blk):
    r, d = xs.shape

    def vec(n):
        return pl.BlockSpec((1, 1, n), lambda i: (layer, 0, 0))

    def res(a):
        return _resident((1,) + a.shape[1:], lambda i: (layer, 0, 0))

    def tab():
        return pl.BlockSpec((tm, LANES), lambda i: (tab_blk(i), 0))

    def tab_t():
        return pl.BlockSpec((LANES, tm), lambda i: (0, tab_blk(i)))

    def rows(n):
        return pl.BlockSpec((tm, n), lambda i: (i, 0))

    hm, hd = MLA_HEADS * LANES, DIFF_HEADS * LANES
    out_shapes = [
        jax.ShapeDtypeStruct((hm, r), BF16),
        jax.ShapeDtypeStruct((r, hm), BF16),
        jax.ShapeDtypeStruct((BRANCH_WIDTH, r), BF16),
        jax.ShapeDtypeStruct((hd, r), BF16),
        jax.ShapeDtypeStruct((r, hd), BF16),
        jax.ShapeDtypeStruct((hd, r), BF16),
        jax.ShapeDtypeStruct((r, BRANCH_WIDTH), F32),
    ]
    return pl.pallas_call(
        _inproj_kernel,
        grid=(n_tiles,),
        in_specs=[
            rows(d),
            pl.BlockSpec((1, 1, N_MOD, d), lambda i: (layer, grp(i), 0, 0)),
            pl.BlockSpec((1, 3, d), lambda i: (layer, 0, 0)),
            res(wt), res(wn), res(gcqt), res(wuqt), res(gckvt), res(wvt),
            vec(gckv.shape[2]), res(wk), res(gqmt), res(gkm), res(gqdt), res(gkd),
            tab(), tab(), tab(), tab(),
            tab_t(), tab_t(), tab_t(), tab_t(),
        ],
        out_specs=[rows(s.shape[1]) if s.shape[0] == r else pl.BlockSpec((s.shape[0], tm), lambda i: (0, i))
                   for s in out_shapes],
        out_shape=out_shapes,
        scratch_shapes=[pltpu.VMEM((wt.shape[2], wt.shape[1]), BF16)],
        compiler_params=_params(("arbitrary",)),
        name="inproj",
    )(xs, mod4, norm_g, wt, wn, gcqt, wuqt, gckvt, wvt, gckv, wk, gqmt, gkm, gqdt, gkd, *tabs, *tabs_t)


KEY_CHUNK = 256
LOGIT_SLOTS = 3


def _col_stat(x, op):
    t, n = x.shape
    r = 64 if t % 64 == 0 else SUBLANES
    return op(op(x.reshape(t // r, r, n), axis=0), axis=0, keepdims=True)


def _flash_heads(n_heads, get_qt, kblk, vrows, k_refs, vt_refs, s_scr, finish):
    depth = s_scr.shape[0]
    chunks = [(k_ref, vt_ref, c0, min(KEY_CHUNK, k_ref.shape[0])) for k_ref, vt_ref in zip(k_refs, vt_refs)
              for c0 in range(0, k_ref.shape[0], min(KEY_CHUNK, k_ref.shape[0]))]
    steps = [(h, ci) for h in range(n_heads) for ci in range(len(chunks))]

    def logits(n):
        h, ci = steps[n]
        k_ref, _, c0, ck = chunks[ci]
        s_scr[n % depth, 0:ck] = _dot(k_ref[c0:c0 + ck, kblk(h)], get_qt(h))

    for n in range(min(depth - 1, len(steps))):
        logits(n)
    m = acc = None
    for n, (h, ci) in enumerate(steps):
        if n + depth - 1 < len(steps):
            logits(n + depth - 1)
        _, vt_ref, c0, ck = chunks[ci]
        s = s_scr[n % depth, 0:ck]
        cm = _col_stat(s, jnp.max)
        m_new = cm if ci == 0 else jnp.maximum(m, cm)
        p = jnp.exp2(s - m_new).astype(BF16)
        vt1 = jnp.concatenate([vt_ref[vrows(h), c0:c0 + ck], jnp.ones((BF16_ROWS, ck), BF16)], axis=0)
        pv = _dot(vt1, p)
        acc = pv if ci == 0 else jnp.exp2(m - m_new) * acc + pv
        m = m_new
        if ci == len(chunks) - 1:
            dv = acc.shape[0] - BF16_ROWS
            finish(h, acc[:dv] * (1.0 / acc[dv:dv + 1]))


def _mla_attn_kernel(*refs, n_seg):
    qt_ref = refs[0]
    k_refs = refs[1:1 + n_seg]
    vt_refs = refs[1 + n_seg:1 + 2 * n_seg]
    o_ref, s_scr = refs[-2:]
    held = []

    def finish(h, ot):
        held.append(ot)
        if h % 2 == 1:
            pair_t = jnp.concatenate(held, axis=0)
            o_ref[:, (h // 2) * LANES:(h // 2 + 1) * LANES] = pair_t.T.astype(BF16)
            held.clear()

    _flash_heads(MLA_HEADS, lambda h: qt_ref[h * LANES:(h + 1) * LANES, :],
                 lambda h: slice(h * LANES, (h + 1) * LANES), lambda h: slice(h * MLA_V, (h + 1) * MLA_V),
                 k_refs, vt_refs, s_scr, finish)


def _diff_attn_kernel(*refs, n_seg, lam_init):
    qt_ref = refs[0]
    k_refs = refs[1:1 + n_seg]
    vt_refs = refs[1 + n_seg:1 + 2 * n_seg]
    lam_ref, gs_ref = refs[1 + 2 * n_seg:3 + 2 * n_seg]
    o_ref, s_scr = refs[-2:]
    bq = qt_ref.shape[1]
    lv = lam_ref[0]
    lam = (jnp.exp(jnp.sum(lv[0:1] * lv[1:2], axis=-1, keepdims=True))
           - jnp.exp(jnp.sum(lv[2:3] * lv[3:4], axis=-1, keepdims=True)) + lam_init)
    gs = gs_ref[0]
    lo = lax.broadcasted_iota(jnp.int32, (LANES, bq), 0) < DIFF_DIM
    blk = lambda h: slice(h * LANES, (h + 1) * LANES)

    def get_qt(h):
        qt = qt_ref[blk(h), :]
        zero = jnp.zeros_like(qt)
        return jnp.concatenate([jnp.where(lo, qt, zero), jnp.where(lo, zero, qt)], axis=1)

    def finish(h, oc):
        o = (oc[:, :bq] - lam * oc[:, bq:]).T
        o_ref[:, blk(h)] = (_rms(o, gs, DIFF_V) * (1.0 - lam_init)).astype(BF16)

    _flash_heads(DIFF_HEADS, get_qt, blk, blk, k_refs, vt_refs, s_scr, finish)


def _attn_call(kern, qt, k, vt, extra, *, width_qk, width_v, q_cols, batch, seq, ctx, r_lat, bq,
               latent, name):
    extra_specs = [pl.BlockSpec((1,) + a.shape[1:], functools.partial(lambda l, b, i: (l, 0, 0), lyr))
                   for a, lyr in extra]
    extra_args = [a for a, _ in extra]
    ctx_blk0 = r_lat // ctx
    if latent:
        nq = seq // bq
        grid = (batch, nq)
        q_spec = pl.BlockSpec((width_qk, bq), lambda b, i: (0, b * nq + i))
        k_specs = [pl.BlockSpec((ctx, width_qk), lambda b, i: (ctx_blk0 + b, 0)),
                   pl.BlockSpec((seq, width_qk), lambda b, i: (b, 0))]
        v_specs = [pl.BlockSpec((width_v, ctx), lambda b, i: (0, ctx_blk0 + b)),
                   pl.BlockSpec((width_v, seq), lambda b, i: (0, b))]
        o_spec = pl.BlockSpec((bq, width_v), lambda b, i: (b * nq + i, 0))
        k_args, v_args, o_rows = [k, k], [vt, vt], r_lat
    else:
        grid = (batch, 1)
        q_spec = pl.BlockSpec((width_qk, ctx), lambda b, i: (0, ctx_blk0 + b))
        k_specs = [pl.BlockSpec((ctx, width_qk), lambda b, i: (ctx_blk0 + b, 0))]
        v_specs = [pl.BlockSpec((width_v, ctx), lambda b, i: (0, ctx_blk0 + b))]
        o_spec = pl.BlockSpec((ctx, width_v), lambda b, i: (b, 0))
        k_args, v_args, o_rows = [k], [vt], batch * ctx
    return pl.pallas_call(
        functools.partial(kern, n_seg=len(k_args)),
        grid=grid,
        in_specs=[q_spec] + k_specs + v_specs + extra_specs,
        out_specs=o_spec,
        out_shape=jax.ShapeDtypeStruct((o_rows, width_v), BF16),
        scratch_shapes=[pltpu.VMEM((LOGIT_SLOTS, KEY_CHUNK, q_cols * q_spec.block_shape[1]), F32)],
        compiler_params=_params(("arbitrary", "arbitrary")),
        name=name,
    )(qt, *k_args, *v_args, *extra_args)


def _merge_kernel(x_ref, mod_ref, ng_ref, wgc_ref, oml_ref, omc_ref, odl_ref, odc_ref, u_ref, up_ref, un_ref,
                  cw_ref, wbr_ref, wo_ref, o_ref, *, n_lat_tiles, seq, ctx):
    i = pl.program_id(0)
    is_lat = i < n_lat_tiles
    tm, d = x_ref.shape
    x = x_ref[...]
    md = mod_ref[0, 0]
    h = _modulate(x, ng_ref[0, 1:2], md[3:4], md[4:5]).astype(BF16)

    u = u_ref[...]
    row = lax.broadcasted_iota(jnp.int32, (tm, 1), 0)
    seq_len = jnp.where(is_lat, seq, ctx)
    pos = (i * tm + row) & (seq_len - 1)
    u_dn = jnp.where(row == 0, up_ref[SUBLANES - 1:SUBLANES, :], pltpu.roll(u, 1, 0))
    u_dn = jnp.where(pos == 0, 0.0, u_dn)
    u_up = jnp.where(row == tm - 1, un_ref[0:1, :], pltpu.roll(u, tm - 1, 0))
    u_up = jnp.where(pos == seq_len - 1, 0.0, u_up)
    cw = cw_ref[0]
    y = cw[0:1] * u_dn + cw[1:2] * u + cw[2:3] * u_up
    conv = (_dot(h, wgc_ref[0, :, 3 * d:]) * y).astype(BF16)

    branches = (jnp.where(is_lat, oml_ref[...], omc_ref[...]), jnp.where(is_lat, odl_ref[...], odc_ref[...]), conv)
    m = None
    for n, yb in enumerate(branches):
        g = _sigmoid(_dot(h, wgc_ref[0, :, n * d:(n + 1) * d]))
        t = g * _dot(yb, wbr_ref[0, n])
        m = t if m is None else m + t
    o_ref[...] = x + md[5:6] * _dot(m.astype(BF16), wo_ref[0])


def _merge_call(xs, mod4, norm_g, wgc, om_lat, om_ctx, od_lat, od_ctx, u, conv_w, wbr, wo, *, layer, n_tiles,
                tm, grp, n_lat_tiles, seq, ctx):
    r, d = xs.shape
    hb = tm // SUBLANES
    last8 = u.shape[0] // SUBLANES - 1

    def rows(n):
        return pl.BlockSpec((tm, n), lambda i: (i, 0))

    lat_rows = pl.BlockSpec((tm, BRANCH_WIDTH), lambda i: (jnp.minimum(i, n_lat_tiles - 1), 0))
    ctx_rows = pl.BlockSpec((tm, BRANCH_WIDTH), lambda i: (jnp.maximum(i - n_lat_tiles, 0), 0))

    kern = functools.partial(_merge_kernel, n_lat_tiles=n_lat_tiles, seq=seq, ctx=ctx)
    return pl.pallas_call(
        kern,
        grid=(n_tiles,),
        in_specs=[
            rows(d),
            pl.BlockSpec((1, 1, N_MOD, d), lambda i: (layer, grp(i), 0, 0)),
            pl.BlockSpec((1, 3, d), lambda i: (layer, 0, 0)),
            _resident((1,) + wgc.shape[1:], lambda i: (layer, 0, 0)),
            lat_rows, ctx_rows, lat_rows, ctx_rows, rows(BRANCH_WIDTH),
            pl.BlockSpec((SUBLANES, BRANCH_WIDTH), lambda i: (jnp.maximum(i * hb - 1, 0), 0)),
            pl.BlockSpec((SUBLANES, BRANCH_WIDTH), lambda i: (jnp.minimum((i + 1) * hb, last8), 0)),
            pl.BlockSpec((1,) + conv_w.shape[1:], lambda i: (layer, 0, 0)),
            _resident((1,) + wbr.shape[1:], lambda i: (layer, 0, 0, 0)),
            _resident((1, d, d), lambda i: (layer, 0, 0)),
        ],
        out_specs=rows(d),
        out_shape=jax.ShapeDtypeStruct((n_tiles * tm, d), F32),
        compiler_params=_params(("arbitrary",)),
        name="merge",
    )(xs, mod4, norm_g, wgc, om_lat, om_ctx, od_lat, od_ctx, u, u, u, conv_w, wbr, wo)


def _rope_tables(seq, tm):
    t = jnp.arange(seq)
    row = (t // GRID_W).astype(F32)[:, None]
    col = (t % GRID_W).astype(F32)[:, None]

    lane = jnp.arange(LANES)

    def axis_tables(rot_dim, rel, live):
        q4 = rot_dim // 4
        inv = ROPE_THETA ** (-jnp.arange(q4, dtype=F32) / q4)
        ang = jnp.where((rel // (2 * q4)) == 0, row, col) * inv[rel % q4][None, :]
        first = ((rel // q4) % 2) == 0
        c = jnp.where(live, jnp.cos(ang), 1.0)
        s = jnp.where(live, jnp.where(first, -jnp.sin(ang), jnp.sin(ang)), 0.0)
        ident = [jnp.ones((tm, LANES), F32), jnp.zeros((tm, LANES), F32)]
        return [jnp.concatenate([a, b], axis=0) for a, b in zip((c, s), ident)]

    mla_live = (lane >= MLA_NOPE) & (lane < MLA_QK)
    mla = axis_tables(MLA_ROPE, jnp.where(mla_live, lane - MLA_NOPE, 0), mla_live)
    diff = axis_tables(DIFF_DIM, lane % DIFF_DIM, lane >= 0)
    return mla + diff


def _pad_last(a, n):
    return jnp.pad(a, [(0, 0)] * (a.ndim - 1) + [(0, n - a.shape[-1])])


def kernel(x, c, ctx, c_ctx, w_mod, b_mod, norm_g, ffn1_up, ffn1_down, ffn2_up, ffn2_down,
           w_in, g_cq, w_uq, g_ckv, w_ukv, g_q_mla, g_k_mla, g_q_diff, g_k_diff, lam,
           g_subln, conv_w, w_br, w_o):
    batch, seq, d = x.shape
    n_ctx = ctx.shape[1]
    depth = w_mod.shape[0]
    r_lat, r_ctx = batch * seq, batch * n_ctx
    tm, tm_ffn = 512, 1024
    bq_mla, bq_diff = 512, 256
    assert seq % tm == 0 and r_ctx % tm == 0 and seq % bq_mla == 0 and r_lat % n_ctx == 0
    assert seq % tm_ffn == 0 and r_ctx % tm_ffn == 0
    assert n_ctx % min(KEY_CHUNK, n_ctx) == 0 and seq % KEY_CHUNK == 0
    assert seq & (seq - 1) == 0 and n_ctx & (n_ctx - 1) == 0 and seq % GRID_W == 0
    assert d == 1024 and w_br.shape[2] == BRANCH_WIDTH
    n_lat_tiles, n_all_tiles = r_lat // tm, (r_lat + r_ctx) // tm
    tiles_per_batch = seq // tm

    def grp_of(tile):
        return lambda i: jnp.where(i < r_lat // tile, i // (seq // tile), batch)

    grp, grp_ffn = grp_of(tm), grp_of(tm_ffn)

    def tab_blk(i):
        return jnp.where(i < n_lat_tiles, i % tiles_per_batch, tiles_per_batch)

    rank_q, rank_kv = g_cq.shape[1], g_ckv.shape[1]
    assert (rank_q, rank_kv) == (_T_CQ[1] - _T_CQ[0], _T_CKV[1] - _T_CKV[0])
    o = [0]
    for n in (rank_q, rank_kv, MLA_ROPE, 512, 512, 512, 512, 512, 512, 3 * d):
        o.append(o[-1] + n)
    seg = lambda k: w_in[:, :, o[k]:o[k + 1]]
    zeros = lambda n: jnp.zeros((depth, d, n), w_in.dtype)
    swap = lambda a: jnp.swapaxes(a, 1, 2)
    lanes = lambda g: jnp.broadcast_to(g[:, :, None], g.shape + (LANES,))
    wt = jnp.concatenate([seg(0), seg(1), seg(3), seg(5)], axis=-1).astype(BF16)
    def partner(a, rot_dim):
        q4 = rot_dim // 4
        g = a.reshape(a.shape[:-1] + (a.shape[-1] // rot_dim, 2, 2, q4))
        return jnp.flip(g, axis=-2).reshape(a.shape)

    def place_rope(a):
        return jnp.concatenate([jnp.zeros(a.shape[:-1] + (MLA_NOPE,), a.dtype), a,
                                jnp.zeros(a.shape[:-1] + (LANES - MLA_QK,), a.dtype)], axis=-1)

    wn = jnp.concatenate([seg(1), place_rope(seg(2)), place_rope(partner(seg(2), MLA_ROPE)),
                          seg(4), partner(seg(4), DIFF_DIM), seg(7), seg(8)], axis=-1).astype(BF16)
    wgc = jnp.concatenate([seg(9), seg(6)], axis=-1).astype(BF16)
    wuq = _pad_last(w_uq.reshape(depth, rank_q, MLA_HEADS, MLA_QK), LANES)
    wuqt = swap(wuq.reshape(depth, rank_q, MLA_HEADS * LANES)).astype(BF16)
    wkv4 = w_ukv.reshape(depth, rank_kv, MLA_HEADS, MLA_NOPE + MLA_V)
    wk = _pad_last(wkv4[..., :MLA_NOPE], LANES).reshape(depth, rank_kv, MLA_HEADS * LANES).astype(BF16)
    wvt = swap(wkv4[..., MLA_NOPE:].reshape(depth, rank_kv, MLA_HEADS * MLA_V)).astype(BF16)
    ffn_w = [(ffn1_up.astype(BF16), ffn1_down.astype(BF16)), (ffn2_up.astype(BF16), ffn2_down.astype(BF16))]
    wbr = w_br.astype(BF16)
    wo = w_o.astype(BF16)
    gcqt = lanes(g_cq)
    gckvt = lanes(g_ckv)
    gckv = g_ckv[:, None, :]
    gqmt = lanes(_pad_last(g_q_mla, LANES))
    gkm = jnp.stack([_pad_last(g_k_mla, LANES),
                     place_rope(partner(g_k_mla[:, MLA_NOPE:], MLA_ROPE))], axis=1)
    gqdt = lanes(jnp.tile(g_q_diff, (1, 2)))
    gkd = jnp.stack([jnp.tile(g_k_diff, (1, 2)), jnp.tile(partner(g_k_diff, DIFF_DIM), (1, 2))], axis=1)
    gsub = g_subln[:, None, :]
    tabs = _rope_tables(seq, tm)
    tabs_t = [t.T for t in tabs]

    cond = jnp.concatenate([c, c_ctx[None, :]], axis=0)
    rows_pad = (batch + 1 + SUBLANES - 1) // SUBLANES * SUBLANES
    cond = jnp.pad(cond, ((0, rows_pad - batch - 1), (0, 0)))
    mod4 = _mod_call(cond, w_mod, b_mod).reshape(depth, rows_pad, N_MOD, d)

    xs = (x.reshape(r_lat, d), ctx.reshape(r_ctx, d))
    attn_kw = dict(batch=batch, seq=seq, ctx=n_ctx, r_lat=r_lat)
    for l in range(depth):
        last = l == depth - 1
        lam_init = 0.8 - 0.6 * math.exp(-0.3 * l)
        tm1, grp1 = (tm, grp) if l == 0 else (tm_ffn, grp_ffn)
        xs = _ffn_call(xs, mod4, norm_g, *ffn_w[0], layer=l, k0=0, gi=0,
                       n_tiles=n_all_tiles * tm // tm1, tm=tm1, grp=grp1)
        qmt, km, vmt, dqt, dk, dvt, u = _inproj_call(
            xs, mod4, norm_g, wt, wn, gcqt, wuqt, gckvt, wvt, gckv, wk, gqmt, gkm, gqdt, gkd, tabs, tabs_t,
            layer=l, n_tiles=n_all_tiles, tm=tm, grp=grp, tab_blk=tab_blk)
        diff_kern = functools.partial(_diff_attn_kernel, lam_init=lam_init)
        diff_extra = [(lam, l), (gsub, l)]
        mla_kw = dict(width_qk=MLA_HEADS * LANES, width_v=BRANCH_WIDTH, q_cols=1, bq=bq_mla, **attn_kw)
        diff_kw = dict(width_qk=DIFF_HEADS * LANES, width_v=BRANCH_WIDTH, q_cols=2, bq=bq_diff, **attn_kw)
        om = _attn_call(_mla_attn_kernel, qmt, km, vmt, [], latent=True, name="mla_lat", **mla_kw)
        od = _attn_call(diff_kern, dqt, dk, dvt, diff_extra, latent=True, name="diff_lat", **diff_kw)
        n_tiles, om_ctx, od_ctx = n_lat_tiles, om, od
        if not last:
            om_ctx = _attn_call(_mla_attn_kernel, qmt, km, vmt, [], latent=False, name="mla_ctx", **mla_kw)
            od_ctx = _attn_call(diff_kern, dqt, dk, dvt, diff_extra, latent=False, name="diff_ctx", **diff_kw)
            n_tiles = n_all_tiles
        xs = _merge_call(xs, mod4, norm_g, wgc, om, om_ctx, od, od_ctx, u, conv_w, wbr, wo, layer=l,
                         n_tiles=n_tiles, tm=tm, grp=grp, n_lat_tiles=n_lat_tiles, seq=seq, ctx=n_ctx)
        xs = _ffn_call(xs, mod4, norm_g, *ffn_w[1], layer=l, k0=6, gi=2,
                       n_tiles=n_tiles * tm // tm_ffn, tm=tm_ffn, grp=grp_ffn)
    return xs.reshape(batch, seq, d)
```

```python
import functools
import math

import jax
import jax.numpy as jnp
from jax import lax
from jax.experimental import pallas as pl
from jax.experimental.pallas import tpu as pltpu

GRID_W = 64
N_MOD = 9
MLA_HEADS = 8
MLA_NOPE = 64
MLA_ROPE = 32
MLA_QK = MLA_NOPE + MLA_ROPE
MLA_V = 64
DIFF_HEADS = 4
DIFF_DIM = 64
DIFF_V = 2 * DIFF_DIM
BRANCH_WIDTH = 512
ROPE_THETA = 10000.0
EPS = 1e-6
LOG2E = math.log2(math.e)

LANES = 128
SUBLANES = 8
BF16_ROWS = 16
MXU_COLS = 256
FFN_CHUNK_ELEMS = 768 * 1024
FFN_SPLIT = 2
VMEM_LIMIT_BYTES = 56 * 1024 * 1024
BF16 = jnp.bfloat16
F32 = jnp.float32


def _dot(a, b):
    return jnp.dot(a, b, preferred_element_type=F32)


def _rms(x, g, n):
    ss = jnp.sum(x * x, axis=-1, keepdims=True)
    return x * lax.rsqrt(ss * (1.0 / n) + EPS) * g


def _modulate(x, g, shift, scale):
    return _rms(x, g, x.shape[-1]) * (1.0 + scale) + shift


def _sigmoid(x):
    return 1.0 / (1.0 + jnp.exp(-x))


def _params(sem):
    return pltpu.CompilerParams(dimension_semantics=sem, vmem_limit_bytes=VMEM_LIMIT_BYTES)


def _resident(block_shape, index_map):
    return pl.BlockSpec(block_shape, index_map, pipeline_mode=pl.Buffered(1))


def _mod_kernel(cond_ref, w_ref, b_ref, o_ref):
    c = cond_ref[...]
    a = (c * _sigmoid(c)).astype(BF16)
    o_ref[0] = _dot(a, w_ref[0].astype(BF16)) + b_ref[0]


def _mod_call(cond, w_mod, b_mod):
    depth, d, nd = w_mod.shape
    rows = cond.shape[0]
    tn = d
    return pl.pallas_call(
        _mod_kernel,
        grid=(depth, nd // tn),
        in_specs=[
            pl.BlockSpec((rows, d), lambda l, j: (0, 0)),
            pl.BlockSpec((1, d, tn), lambda l, j: (l, 0, j)),
            pl.BlockSpec((1, 1, tn), lambda l, j: (l, 0, j)),
        ],
        out_specs=pl.BlockSpec((1, rows, tn), lambda l, j: (l, 0, j)),
        out_shape=jax.ShapeDtypeStruct((depth, rows, nd), F32),
        compiler_params=_params(("arbitrary", "arbitrary")),
        name="mod",
    )(cond, w_mod, b_mod.reshape(depth, 1, nd))


def _ffn_kernel(*refs, k0, gi, dff, chunks, n_first):
    x_refs, (mod_ref, ng_ref, wup_ref, wdn_ref, o_ref, u_scr) = refs[:-6], refs[-6:]
    md = mod_ref[0, 0]
    tn = o_ref.shape[0] // FFN_SPLIT
    for t0 in range(0, o_ref.shape[0], tn):
        ts = slice(t0, t0 + tn)
        x = x_refs[0][ts, :]
        if n_first is not None:
            x = jnp.where(pl.program_id(0) < n_first, x, x_refs[1][ts, :])
        h = _modulate(x, ng_ref[0, gi:gi + 1], md[k0:k0 + 1], md[k0 + 1:k0 + 2]).astype(BF16)
        for lo, hi in chunks:
            a = _dot(h, wup_ref[0, :, lo:hi])
            b = _dot(h, wup_ref[0, :, dff + lo:dff + hi])
            u_scr[ts, lo:hi] = (a * _sigmoid(a) * b).astype(BF16)
        y = _dot(u_scr[ts, :], wdn_ref[0])
        o_ref[ts, :] = x + (0.5 * md[k0 + 2:k0 + 3]) * y


def _ffn_call(xs, mod4, norm_g, wup, wdn, *, layer, k0, gi, n_tiles, tm, grp):
    xs = xs if isinstance(xs, tuple) else (xs,)
    d = xs[0].shape[1]
    dff = wdn.shape[1]
    width = max(MXU_COLS, FFN_CHUNK_ELEMS // (tm // FFN_SPLIT) // MXU_COLS * MXU_COLS)
    chunks = tuple((lo, min(lo + width, dff)) for lo in range(0, dff, width))
    n_first = xs[0].shape[0] // tm if len(xs) == 2 else None
    kern = functools.partial(_ffn_kernel, k0=k0, gi=gi, dff=dff, chunks=chunks, n_first=n_first)
    if n_first is None:
        x_specs = [pl.BlockSpec((tm, d), lambda i: (i, 0))]
    else:
        x_specs = [pl.BlockSpec((tm, d), lambda i: (jnp.minimum(i, n_first - 1), 0)),
                   pl.BlockSpec((tm, d), lambda i: (jnp.maximum(i - n_first, 0), 0))]
    return pl.pallas_call(
        kern,
        grid=(n_tiles,),
        in_specs=x_specs + [
            pl.BlockSpec((1, 1, N_MOD, d), lambda i: (layer, grp(i), 0, 0)),
            pl.BlockSpec((1, 3, d), lambda i: (layer, 0, 0)),
            _resident((1, d, 2 * dff), lambda i: (layer, 0, 0)),
            _resident((1, dff, d), lambda i: (layer, 0, 0)),
        ],
        out_specs=pl.BlockSpec((tm, d), lambda i: (i, 0)),
        out_shape=jax.ShapeDtypeStruct((n_tiles * tm, d), F32),
        scratch_shapes=[pltpu.VMEM((tm, dff), BF16)],
        compiler_params=_params(("arbitrary",)),
        name="ffn",
    )(*xs, mod4, norm_g, wup, wdn)


_T_CQ = (0, 384)
_T_CKV = (384, 640)
_T_DQ = (640, 1152)
_T_DV = (1152, 1664)
_N_CKV = (0, 256)
_N_KR = (256, 384)
_N_DK = (384, 896)
_N_CC = (896, 1408)
_N_CX = (1408, 1920)


def _partner_lanes(x, q4):
    first = (lax.broadcasted_iota(jnp.int32, x.shape, 1) & q4) == 0
    return jnp.where(first, pltpu.roll(x, LANES - q4, 1), pltpu.roll(x, q4, 1))


INPROJ_SPLIT = 2


def _tile_lanes(g, n):
    return jnp.concatenate([g] * (n // LANES), axis=1)


def _rms_t(x, g, n):
    ss = jnp.sum(x * x, axis=0, keepdims=True)
    return x * lax.rsqrt(ss * (1.0 / n) + EPS) * g


def _swap_row_blocks(x, width):
    parts = []
    for r0 in range(0, x.shape[0], 2 * width):
        parts += [x[r0 + width:r0 + 2 * width], x[r0:r0 + width]]
    return jnp.concatenate(parts, axis=0)


def _inproj_kernel(x_ref, mod_ref, ng_ref, wt_ref, wn_ref, gcqt_ref, wuqt_ref, gckvt_ref, wvt_ref,
                   gckv_ref, wk_ref, gqmt_ref, gkm_ref, gqdt_ref, gkd_ref,
                   cm_ref, sgm_ref, cd_ref, sgd_ref, cmt_ref, sgmt_ref, cdt_ref, sgdt_ref,
                   qmt_ref, km_ref, vmt_ref, dqt_ref, dk_ref, dvt_ref, u_ref, wt_scr):
    @pl.when(pl.program_id(0) == 0)
    def _():
        wt_scr[...] = wt_ref[0].T

    md = mod_ref[0, 0]
    tn = x_ref.shape[0] // INPROJ_SPLIT
    for t0 in range(0, x_ref.shape[0], tn):
        ts = slice(t0, t0 + tn)
        hf = _modulate(x_ref[ts, :], ng_ref[0, 1:2], md[3:4], md[4:5])
        h = hf.astype(BF16)
        ht = hf.T.astype(BF16)

        pt = _dot(wt_scr[...], ht)
        pn = _dot(h, wn_ref[0])

        def rows(seg):
            return pt[seg[0]:seg[1]]

        def cols(seg):
            return pn[:, seg[0]:seg[1]]

        cqn_t = _rms_t(rows(_T_CQ), _tile_lanes(gcqt_ref[0], tn), _T_CQ[1] - _T_CQ[0]).astype(BF16)
        q_raw_t = _dot(wuqt_ref[0], cqn_t)
        ckvn_t = _rms_t(rows(_T_CKV), _tile_lanes(gckvt_ref[0], tn), _T_CKV[1] - _T_CKV[0]).astype(BF16)
        vmt_ref[:, ts] = _dot(wvt_ref[0], ckvn_t).astype(BF16)
        dvt_ref[:, ts] = rows(_T_DV).astype(BF16)

        cmt, sgmt = cmt_ref[:, ts], sgmt_ref[:, ts]
        gqmt = _tile_lanes(gqmt_ref[0], tn)
        q_scale = MLA_QK ** -0.5 * LOG2E
        for hd in range(MLA_HEADS):
            blk = slice(hd * LANES, (hd + 1) * LANES)
            qn = _rms_t(q_raw_t[blk], gqmt, MLA_QK)
            qr = qn * cmt + _swap_row_blocks(qn, MLA_ROPE // 4) * sgmt
            qmt_ref[blk, ts] = (qr * q_scale).astype(BF16)

        cdt, sgdt = cdt_ref[:, ts], sgdt_ref[:, ts]
        gqdt = _tile_lanes(gqdt_ref[0], tn)
        d_scale = DIFF_DIM ** -0.5 * LOG2E
        dq_t = rows(_T_DQ)
        for hd in range(DIFF_HEADS):
            halves = []
            for comp in range(2):
                r0 = hd * LANES + comp * DIFF_DIM
                xc = dq_t[r0:r0 + DIFF_DIM]
                ss = jnp.sum(xc * xc, axis=0, keepdims=True)
                halves.append(xc * lax.rsqrt(ss * (1.0 / DIFF_DIM) + EPS))
            xn = jnp.concatenate(halves, axis=0) * gqdt
            xr = xn * cdt + _swap_row_blocks(xn, DIFF_DIM // 4) * sgdt
            dqt_ref[hd * LANES:(hd + 1) * LANES, ts] = (xr * d_scale).astype(BF16)

        ckvn = _rms(cols(_N_CKV), gckv_ref[0], _N_CKV[1] - _N_CKV[0]).astype(BF16)
        k_raw = _dot(ckvn, wk_ref[0])
        krope = cols(_N_KR)
        gc_m = gkm_ref[0, 0:1] * cm_ref[ts, :]
        rot_m = _partner_lanes(krope, MLA_ROPE // 4) * (gkm_ref[0, 1:2] * sgm_ref[ts, :])
        for hd in range(MLA_HEADS):
            blk = slice(hd * LANES, (hd + 1) * LANES)
            kf = k_raw[:, blk] + krope
            r = lax.rsqrt(jnp.sum(kf * kf, axis=-1, keepdims=True) * (1.0 / MLA_QK) + EPS)
            km_ref[ts, blk] = ((kf * gc_m + rot_m) * r).astype(BF16)

        gc_d = gkd_ref[0, 0:1] * cd_ref[ts, :]
        gs_d = gkd_ref[0, 1:2] * sgd_ref[ts, :]
        lo = lax.broadcasted_iota(jnp.int32, (tn, LANES), 1) < DIFF_DIM
        dk = cols(_N_DK)
        for hd in range(DIFF_HEADS):
            blk = slice(hd * LANES, (hd + 1) * LANES)
            xh = dk[:, blk]
            sq = xh * xh
            s_lo = jnp.sum(jnp.where(lo, sq, 0.0), axis=-1, keepdims=True)
            s_hi = jnp.sum(jnp.where(lo, 0.0, sq), axis=-1, keepdims=True)
            r = lax.rsqrt(jnp.where(lo, s_lo, s_hi) * (1.0 / DIFF_DIM) + EPS)
            dk_ref[ts, blk] = ((xh * gc_d + _partner_lanes(xh, DIFF_DIM // 4) * gs_d) * r).astype(BF16)

        u_ref[ts, :] = cols(_N_CC) * cols(_N_CX)


def _inproj_call(xs, mod4, norm_g, wt, wn, gcqt, wuqt, gckvt, wvt, gckv, wk, gqmt, gkm, gqdt, gkd, tabs, tabs_t,
                 *, layer, n_tiles, tm, grp, tab_blk):
    r, d = xs.shape

    def vec(n):
        return pl.BlockSpec((1, 1, n), lambda i: (layer, 0, 0))

    def res(a):
        return _resident((1,) + a.shape[1:], lambda i: (layer, 0, 0))

    def tab():
        return pl.BlockSpec((tm, LANES), lambda i: (tab_blk(i), 0))

    def tab_t():
        return pl.BlockSpec((LANES, tm), lambda i: (0, tab_blk(i)))

    def rows(n):
        return pl.BlockSpec((tm, n), lambda i: (i, 0))

    hm, hd = MLA_HEADS * LANES, DIFF_HEADS * LANES
    out_shapes = [
        jax.ShapeDtypeStruct((hm, r), BF16),
        jax.ShapeDtypeStruct((r, hm), BF16),
        jax.ShapeDtypeStruct((BRANCH_WIDTH, r), BF16),
        jax.ShapeDtypeStruct((hd, r), BF16),
        jax.ShapeDtypeStruct((r, hd), BF16),
        jax.ShapeDtypeStruct((hd, r), BF16),
        jax.ShapeDtypeStruct((r, BRANCH_WIDTH), F32),
    ]
    return pl.pallas_call(
        _inproj_kernel,
        grid=(n_tiles,),
        in_specs=[
            rows(d),
            pl.BlockSpec((1, 1, N_MOD, d), lambda i: (layer, grp(i), 0, 0)),
            pl.BlockSpec((1, 3, d), lambda i: (layer, 0, 0)),
            res(wt), res(wn), res(gcqt), res(wuqt), res(gckvt), res(wvt),
            vec(gckv.shape[2]), res(wk), res(gqmt), res(gkm), res(gqdt), res(gkd),
            tab(), tab(), tab(), tab(),
            tab_t(), tab_t(), tab_t(), tab_t(),
        ],
        out_specs=[rows(s.shape[1]) if s.shape[0] == r else pl.BlockSpec((s.shape[0], tm), lambda i: (0, i))
                   for s in out_shapes],
        out_shape=out_shapes,
        scratch_shapes=[pltpu.VMEM((wt.shape[2], wt.shape[1]), BF16)],
        compiler_params=_params(("arbitrary",)),
        name="inproj",
    )(xs, mod4, norm_g, wt, wn, gcqt, wuqt, gckvt, wvt, gckv, wk, gqmt, gkm, gqdt, gkd, *tabs, *tabs_t)


KEY_CHUNK = 256
LOGIT_SLOTS = 3


def _col_stat(x, op):
    t, n = x.shape
    r = 64 if t % 64 == 0 else SUBLANES
    return op(op(x.reshape(t // r, r, n), axis=0), axis=0, keepdims=True)


def _flash_heads(n_heads, get_qt, kblk, vrows, k_refs, vt_refs, s_scr, finish):
    depth = s_scr.shape[0]
    chunks = [(k_ref, vt_ref, c0, min(KEY_CHUNK, k_ref.shape[0])) for k_ref, vt_ref in zip(k_refs, vt_refs)
              for c0 in range(0, k_ref.shape[0], min(KEY_CHUNK, k_ref.shape[0]))]
    steps = [(h, ci) for h in range(n_heads) for ci in range(len(chunks))]

    def logits(n):
        h, ci = steps[n]
        k_ref, _, c0, ck = chunks[ci]
        s_scr[n % depth, 0:ck] = _dot(k_ref[c0:c0 + ck, kblk(h)], get_qt(h))

    for n in range(min(depth - 1, len(steps))):
        logits(n)
    m = acc = None
    for n, (h, ci) in enumerate(steps):
        if n + depth - 1 < len(steps):
            logits(n + depth - 1)
        _, vt_ref, c0, ck = chunks[ci]
        s = s_scr[n % depth, 0:ck]
        cm = _col_stat(s, jnp.max)
        m_new = cm if ci == 0 else jnp.maximum(m, cm)
        p = jnp.exp2(s - m_new).astype(BF16)
        vt1 = jnp.concatenate([vt_ref[vrows(h), c0:c0 + ck], jnp.ones((BF16_ROWS, ck), BF16)], axis=0)
        pv = _dot(vt1, p)
        acc = pv if ci == 0 else jnp.exp2(m - m_new) * acc + pv
        m = m_new
        if ci == len(chunks) - 1:
            dv = acc.shape[0] - BF16_ROWS
            finish(h, acc[:dv] * (1.0 / acc[dv:dv + 1]))


def _mla_attn_kernel(*refs, n_seg):
    qt_ref = refs[0]
    k_refs = refs[1:1 + n_seg]
    vt_refs = refs[1 + n_seg:1 + 2 * n_seg]
    o_ref, s_scr = refs[-2:]
    held = []

    def finish(h, ot):
        held.append(ot)
        if h % 2 == 1:
            pair_t = jnp.concatenate(held, axis=0)
            o_ref[:, (h // 2) * LANES:(h // 2 + 1) * LANES] = pair_t.T.astype(BF16)
            held.clear()

    _flash_heads(MLA_HEADS, lambda h: qt_ref[h * LANES:(h + 1) * LANES, :],
                 lambda h: slice(h * LANES, (h + 1) * LANES), lambda h: slice(h * MLA_V, (h + 1) * MLA_V),
                 k_refs, vt_refs, s_scr, finish)


def _diff_attn_kernel(*refs, n_seg, lam_init):
    qt_ref = refs[0]
    k_refs = refs[1:1 + n_seg]
    vt_refs = refs[1 + n_seg:1 + 2 * n_seg]
    lam_ref, gs_ref = refs[1 + 2 * n_seg:3 + 2 * n_seg]
    o_ref, s_scr = refs[-2:]
    bq = qt_ref.shape[1]
    lv = lam_ref[0]
    lam = (jnp.exp(jnp.sum(lv[0:1] * lv[1:2], axis=-1, keepdims=True))
           - jnp.exp(jnp.sum(lv[2:3] * lv[3:4], axis=-1, keepdims=True)) + lam_init)
    gs = gs_ref[0]
    lo = lax.broadcasted_iota(jnp.int32, (LANES, bq), 0) < DIFF_DIM
    blk = lambda h: slice(h * LANES, (h + 1) * LANES)

    def get_qt(h):
        qt = qt_ref[blk(h), :]
        zero = jnp.zeros_like(qt)
        return jnp.concatenate([jnp.where(lo, qt, zero), jnp.where(lo, zero, qt)], axis=1)

    def finish(h, oc):
        o = (oc[:, :bq] - lam * oc[:, bq:]).T
        o_ref[:, blk(h)] = (_rms(o, gs, DIFF_V) * (1.0 - lam_init)).astype(BF16)

    _flash_heads(DIFF_HEADS, get_qt, blk, blk, k_refs, vt_refs, s_scr, finish)


def _attn_call(kern, qt, k, vt, extra, *, width_qk, width_v, q_cols, batch, seq, ctx, r_lat, bq,
               latent, name):
    extra_specs = [pl.BlockSpec((1,) + a.shape[1:], functools.partial(lambda l, b, i: (l, 0, 0), lyr))
                   for a, lyr in extra]
    extra_args = [a for a, _ in extra]
    ctx_blk0 = r_lat // ctx
    if latent:
        nq = seq // bq
        grid = (batch, nq)
        q_spec = pl.BlockSpec((width_qk, bq), lambda b, i: (0, b * nq + i))
        k_specs = [pl.BlockSpec((ctx, width_qk), lambda b, i: (ctx_blk0 + b, 0)),
                   pl.BlockSpec((seq, width_qk), lambda b, i: (b, 0))]
        v_specs = [pl.BlockSpec((width_v, ctx), lambda b, i: (0, ctx_blk0 + b)),
                   pl.BlockSpec((width_v, seq), lambda b, i: (0, b))]
        o_spec = pl.BlockSpec((bq, width_v), lambda b, i: (b * nq + i, 0))
        k_args, v_args, o_rows = [k, k], [vt, vt], r_lat
    else:
        grid = (batch, 1)
        q_spec = pl.BlockSpec((width_qk, ctx), lambda b, i: (0, ctx_blk0 + b))
        k_specs = [pl.BlockSpec((ctx, width_qk), lambda b, i: (ctx_blk0 + b, 0))]
        v_specs = [pl.BlockSpec((width_v, ctx), lambda b, i: (0, ctx_blk0 + b))]
        o_spec = pl.BlockSpec((ctx, width_v), lambda b, i: (b, 0))
        k_args, v_args, o_rows = [k], [vt], batch * ctx
    return pl.pallas_call(
        functools.partial(kern, n_seg=len(k_args)),
        grid=grid,
        in_specs=[q_spec] + k_specs + v_specs + extra_specs,
        out_specs=o_spec,
        out_shape=jax.ShapeDtypeStruct((o_rows, width_v), BF16),
        scratch_shapes=[pltpu.VMEM((LOGIT_SLOTS, KEY_CHUNK, q_cols * q_spec.block_shape[1]), F32)],
        compiler_params=_params(("arbitrary", "arbitrary")),
        name=name,
    )(qt, *k_args, *v_args, *extra_args)


def _merge_kernel(x_ref, mod_ref, ng_ref, wgc_ref, oml_ref, omc_ref, odl_ref, odc_ref, u_ref, up_ref, un_ref,
                  cw_ref, wbr_ref, wo_ref, o_ref, *, n_lat_tiles, seq, ctx):
    i = pl.program_id(0)
    is_lat = i < n_lat_tiles
    tm, d = x_ref.shape
    x = x_ref[...]
    md = mod_ref[0, 0]
    h = _modulate(x, ng_ref[0, 1:2], md[3:4], md[4:5]).astype(BF16)

    u = u_ref[...]
    row = lax.broadcasted_iota(jnp.int32, (tm, 1), 0)
    seq_len = jnp.where(is_lat, seq, ctx)
    pos = (i * tm + row) & (seq_len - 1)
    u_dn = jnp.where(row == 0, up_ref[SUBLANES - 1:SUBLANES, :], pltpu.roll(u, 1, 0))
    u_dn = jnp.where(pos == 0, 0.0, u_dn)
    u_up = jnp.where(row == tm - 1, un_ref[0:1, :], pltpu.roll(u, tm - 1, 0))
    u_up = jnp.where(pos == seq_len - 1, 0.0, u_up)
    cw = cw_ref[0]
    y = cw[0:1] * u_dn + cw[1:2] * u + cw[2:3] * u_up
    conv = (_dot(h, wgc_ref[0, :, 3 * d:]) * y).astype(BF16)

    branches = (jnp.where(is_lat, oml_ref[...], omc_ref[...]), jnp.where(is_lat, odl_ref[...], odc_ref[...]), conv)
    m = None
    for n, yb in enumerate(branches):
        g = _sigmoid(_dot(h, wgc_ref[0, :, n * d:(n + 1) * d]))
        t = g * _dot(yb, wbr_ref[0, n])
        m = t if m is None else m + t
    o_ref[...] = x + md[5:6] * _dot(m.astype(BF16), wo_ref[0])


def _merge_call(xs, mod4, norm_g, wgc, om_lat, om_ctx, od_lat, od_ctx, u, conv_w, wbr, wo, *, layer, n_tiles,
                tm, grp, n_lat_tiles, seq, ctx):
    r, d = xs.shape
    hb = tm // SUBLANES
    last8 = u.shape[0] // SUBLANES - 1

    def rows(n):
        return pl.BlockSpec((tm, n), lambda i: (i, 0))

    lat_rows = pl.BlockSpec((tm, BRANCH_WIDTH), lambda i: (jnp.minimum(i, n_lat_tiles - 1), 0))
    ctx_rows = pl.BlockSpec((tm, BRANCH_WIDTH), lambda i: (jnp.maximum(i - n_lat_tiles, 0), 0))

    kern = functools.partial(_merge_kernel, n_lat_tiles=n_lat_tiles, seq=seq, ctx=ctx)
    return pl.pallas_call(
        kern,
        grid=(n_tiles,),
        in_specs=[
            rows(d),
            pl.BlockSpec((1, 1, N_MOD, d), lambda i: (layer, grp(i), 0, 0)),
            pl.BlockSpec((1, 3, d), lambda i: (layer, 0, 0)),
            _resident((1,) + wgc.shape[1:], lambda i: (layer, 0, 0)),
            lat_rows, ctx_rows, lat_rows, ctx_rows, rows(BRANCH_WIDTH),
            pl.BlockSpec((SUBLANES, BRANCH_WIDTH), lambda i: (jnp.maximum(i * hb - 1, 0), 0)),
            pl.BlockSpec((SUBLANES, BRANCH_WIDTH), lambda i: (jnp.minimum((i + 1) * hb, last8), 0)),
            pl.BlockSpec((1,) + conv_w.shape[1:], lambda i: (layer, 0, 0)),
            _resident((1,) + wbr.shape[1:], lambda i: (layer, 0, 0, 0)),
            _resident((1, d, d), lambda i: (layer, 0, 0)),
        ],
        out_specs=rows(d),
        out_shape=jax.ShapeDtypeStruct((n_tiles * tm, d), F32),
        compiler_params=_params(("arbitrary",)),
        name="merge",
    )(xs, mod4, norm_g, wgc, om_lat, om_ctx, od_lat, od_ctx, u, u, u, conv_w, wbr, wo)


def _rope_tables(seq, tm):
    t = jnp.arange(seq)
    row = (t // GRID_W).astype(F32)[:, None]
    col = (t % GRID_W).astype(F32)[:, None]

    lane = jnp.arange(LANES)

    def axis_tables(rot_dim, rel, live):
        q4 = rot_dim // 4
        inv = ROPE_THETA ** (-jnp.arange(q4, dtype=F32) / q4)
        ang = jnp.where((rel // (2 * q4)) == 0, row, col) * inv[rel % q4][None, :]
        first = ((rel // q4) % 2) == 0
        c = jnp.where(live, jnp.cos(ang), 1.0)
        s = jnp.where(live, jnp.where(first, -jnp.sin(ang), jnp.sin(ang)), 0.0)
        ident = [jnp.ones((tm, LANES), F32), jnp.zeros((tm, LANES), F32)]
        return [jnp.concatenate([a, b], axis=0) for a, b in zip((c, s), ident)]

    mla_live = (lane >= MLA_NOPE) & (lane < MLA_QK)
    mla = axis_tables(MLA_ROPE, jnp.where(mla_live, lane - MLA_NOPE, 0), mla_live)
    diff = axis_tables(DIFF_DIM, lane % DIFF_DIM, lane >= 0)
    return mla + diff


def _pad_last(a, n):
    return jnp.pad(a, [(0, 0)] * (a.ndim - 1) + [(0, n - a.shape[-1])])


def kernel(x, c, ctx, c_ctx, w_mod, b_mod, norm_g, ffn1_up, ffn1_down, ffn2_up, ffn2_down,
           w_in, g_cq, w_uq, g_ckv, w_ukv, g_q_mla, g_k_mla, g_q_diff, g_k_diff, lam,
           g_subln, conv_w, w_br, w_o):
    batch, seq, d = x.shape
    n_ctx = ctx.shape[1]
    depth = w_mod.shape[0]
    r_lat, r_ctx = batch * seq, batch * n_ctx
    tm, tm_ffn = 512, 1024
    bq_mla, bq_diff = 512, 512
    assert seq % tm == 0 and r_ctx % tm == 0 and seq % bq_mla == 0 and r_lat % n_ctx == 0
    assert seq % tm_ffn == 0 and r_ctx % tm_ffn == 0
    assert n_ctx % min(KEY_CHUNK, n_ctx) == 0 and seq % KEY_CHUNK == 0
    assert seq & (seq - 1) == 0 and n_ctx & (n_ctx - 1) == 0 and seq % GRID_W == 0
    assert d == 1024 and w_br.shape[2] == BRANCH_WIDTH
    n_lat_tiles, n_all_tiles = r_lat // tm, (r_lat + r_ctx) // tm
    tiles_per_batch = seq // tm

    def grp_of(tile):
        return lambda i: jnp.where(i < r_lat // tile, i // (seq // tile), batch)

    grp, grp_ffn = grp_of(tm), grp_of(tm_ffn)

    def tab_blk(i):
        return jnp.where(i < n_lat_tiles, i % tiles_per_batch, tiles_per_batch)

    rank_q, rank_kv = g_cq.shape[1], g_ckv.shape[1]
    assert (rank_q, rank_kv) == (_T_CQ[1] - _T_CQ[0], _T_CKV[1] - _T_CKV[0])
    o = [0]
    for n in (rank_q, rank_kv, MLA_ROPE, 512, 512, 512, 512, 512, 512, 3 * d):
        o.append(o[-1] + n)
    seg = lambda k: w_in[:, :, o[k]:o[k + 1]]
    zeros = lambda n: jnp.zeros((depth, d, n), w_in.dtype)
    swap = lambda a: jnp.swapaxes(a, 1, 2)
    lanes = lambda g: jnp.broadcast_to(g[:, :, None], g.shape + (LANES,))
    wt = jnp.concatenate([seg(0), seg(1), seg(3), seg(5)], axis=-1).astype(BF16)
    def partner(a, rot_dim):
        q4 = rot_dim // 4
        g = a.reshape(a.shape[:-1] + (a.shape[-1] // rot_dim, 2, 2, q4))
        return jnp.flip(g, axis=-2).reshape(a.shape)

    def place_rope(a):
        return jnp.concatenate([jnp.zeros(a.shape[:-1] + (MLA_NOPE,), a.dtype), a,
                                jnp.zeros(a.shape[:-1] + (LANES - MLA_QK,), a.dtype)], axis=-1)

    wn = jnp.concatenate([seg(1), place_rope(seg(2)), seg(4), seg(7), seg(8)], axis=-1).astype(BF16)
    wgc = jnp.concatenate([seg(9), seg(6)], axis=-1).astype(BF16)
    wuq = _pad_last(w_uq.reshape(depth, rank_q, MLA_HEADS, MLA_QK), LANES)
    wuqt = swap(wuq.reshape(depth, rank_q, MLA_HEADS * LANES)).astype(BF16)
    wkv4 = w_ukv.reshape(depth, rank_kv, MLA_HEADS, MLA_NOPE + MLA_V)
    wk = _pad_last(wkv4[..., :MLA_NOPE], LANES).reshape(depth, rank_kv, MLA_HEADS * LANES).astype(BF16)
    wvt = swap(wkv4[..., MLA_NOPE:].reshape(depth, rank_kv, MLA_HEADS * MLA_V)).astype(BF16)
    ffn_w = [(ffn1_up.astype(BF16), ffn1_down.astype(BF16)), (ffn2_up.astype(BF16), ffn2_down.astype(BF16))]
    wbr = w_br.astype(BF16)
    wo = w_o.astype(BF16)
    gcqt = lanes(g_cq)
    gckvt = lanes(g_ckv)
    gckv = g_ckv[:, None, :]
    gqmt = lanes(_pad_last(g_q_mla, LANES))
    gkm = jnp.stack([_pad_last(g_k_mla, LANES),
                     place_rope(partner(g_k_mla[:, MLA_NOPE:], MLA_ROPE))], axis=1)
    gqdt = lanes(jnp.tile(g_q_diff, (1, 2)))
    gkd = jnp.stack([jnp.tile(g_k_diff, (1, 2)), jnp.tile(partner(g_k_diff, DIFF_DIM), (1, 2))], axis=1)
    gsub = g_subln[:, None, :]
    tabs = _rope_tables(seq, tm)
    tabs_t = [t.T for t in tabs]

    cond = jnp.concatenate([c, c_ctx[None, :]], axis=0)
    rows_pad = (batch + 1 + SUBLANES - 1) // SUBLANES * SUBLANES
    cond = jnp.pad(cond, ((0, rows_pad - batch - 1), (0, 0)))
    mod4 = _mod_call(cond, w_mod, b_mod).reshape(depth, rows_pad, N_MOD, d)

    xs = (x.reshape(r_lat, d), ctx.reshape(r_ctx, d))
    attn_kw = dict(batch=batch, seq=seq, ctx=n_ctx, r_lat=r_lat)
    for l in range(depth):
        last = l == depth - 1
        lam_init = 0.8 - 0.6 * math.exp(-0.3 * l)
        tm1, grp1 = (tm, grp) if l == 0 else (tm_ffn, grp_ffn)
        xs = _ffn_call(xs, mod4, norm_g, *ffn_w[0], layer=l, k0=0, gi=0,
                       n_tiles=n_all_tiles * tm // tm1, tm=tm1, grp=grp1)
        qmt, km, vmt, dqt, dk, dvt, u = _inproj_call(
            xs, mod4, norm_g, wt, wn, gcqt, wuqt, gckvt, wvt, gckv, wk, gqmt, gkm, gqdt, gkd, tabs, tabs_t,
            layer=l, n_tiles=n_all_tiles, tm=tm, grp=grp, tab_blk=tab_blk)
        diff_kern = functools.partial(_diff_attn_kernel, lam_init=lam_init)
        diff_extra = [(lam, l), (gsub, l)]
        mla_kw = dict(width_qk=MLA_HEADS * LANES, width_v=BRANCH_WIDTH, q_cols=1, bq=bq_mla, **attn_kw)
        diff_kw = dict(width_qk=DIFF_HEADS * LANES, width_v=BRANCH_WIDTH, q_cols=2, bq=bq_diff, **attn_kw)
        om = _attn_call(_mla_attn_kernel, qmt, km, vmt, [], latent=True, name="mla_lat", **mla_kw)
        od = _attn_call(diff_kern, dqt, dk, dvt, diff_extra, latent=True, name="diff_lat", **diff_kw)
        n_tiles, om_ctx, od_ctx = n_lat_tiles, om, od
        if not last:
            om_ctx = _attn_call(_mla_attn_kernel, qmt, km, vmt, [], latent=False, name="mla_ctx", **mla_kw)
            od_ctx = _attn_call(diff_kern, dqt, dk, dvt, diff_extra, latent=False, name="diff_ctx", **diff_kw)
            n_tiles = n_all_tiles
        xs = _merge_call(xs, mod4, norm_g, wgc, om, om_ctx, od, od_ctx, u, conv_w, wbr, wo, layer=l,
                         n_tiles=n_tiles, tm=tm, grp=grp, n_lat_tiles=n_lat_tiles, seq=seq, ctx=n_ctx)
        xs = _ffn_call(xs, mod4, norm_g, *ffn_w[1], layer=l, k0=6, gi=2,
                       n_tiles=n_tiles * tm // tm_ffn, tm=tm_ffn, grp=grp_ffn)
    return xs.reshape(batch, seq, d)
```

```python
import functools
import math

import jax
import jax.numpy as jnp
from jax import lax
from jax.experimental import pallas as pl
from jax.experimental.pallas import tpu as pltpu

GRID_W = 64
N_MOD = 9
MLA_HEADS = 8
MLA_NOPE = 64
MLA_ROPE = 32
MLA_QK = MLA_NOPE + MLA_ROPE
MLA_V = 64
DIFF_HEADS = 4
DIFF_DIM = 64
DIFF_V = 2 * DIFF_DIM
BRANCH_WIDTH = 512
ROPE_THETA = 10000.0
EPS = 1e-6
LOG2E = math.log2(math.e)

LANES = 128
SUBLANES = 8
BF16_ROWS = 16
MXU_COLS = 256
FFN_CHUNK_ELEMS = 768 * 1024
FFN_SPLIT = 2
VMEM_LIMIT_BYTES = 56 * 1024 * 1024
BF16 = jnp.bfloat16
F32 = jnp.float32


def _dot(a, b):
    return jnp.dot(a, b, preferred_element_type=F32)


def _rms(x, g, n):
    ss = jnp.sum(x * x, axis=-1, keepdims=True)
    return x * lax.rsqrt(ss * (1.0 / n) + EPS) * g


def _modulate(x, g, shift, scale):
    return _rms(x, g, x.shape[-1]) * (1.0 + scale) + shift


def _sigmoid(x):
    return 1.0 / (1.0 + jnp.exp(-x))


def _params(sem):
    return pltpu.CompilerParams(dimension_semantics=sem, vmem_limit_bytes=VMEM_LIMIT_BYTES)


def _resident(block_shape, index_map):
    return pl.BlockSpec(block_shape, index_map, pipeline_mode=pl.Buffered(1))


def _mod_kernel(cond_ref, w_ref, b_ref, o_ref):
    c = cond_ref[...]
    a = (c * _sigmoid(c)).astype(BF16)
    o_ref[0] = _dot(a, w_ref[0].astype(BF16)) + b_ref[0]


def _mod_call(cond, w_mod, b_mod):
    depth, d, nd = w_mod.shape
    rows = cond.shape[0]
    tn = d
    return pl.pallas_call(
        _mod_kernel,
        grid=(depth, nd // tn),
        in_specs=[
            pl.BlockSpec((rows, d), lambda l, j: (0, 0)),
            pl.BlockSpec((1, d, tn), lambda l, j: (l, 0, j)),
            pl.BlockSpec((1, 1, tn), lambda l, j: (l, 0, j)),
        ],
        out_specs=pl.BlockSpec((1, rows, tn), lambda l, j: (l, 0, j)),
        out_shape=jax.ShapeDtypeStruct((depth, rows, nd), F32),
        compiler_params=_params(("arbitrary", "arbitrary")),
        name="mod",
    )(cond, w_mod, b_mod.reshape(depth, 1, nd))


def _ffn_kernel(*refs, k0, gi, dff, chunks, n_first):
    x_refs, (mod_ref, ng_ref, wup_ref, wdn_ref, o_ref, u_scr) = refs[:-6], refs[-6:]
    md = mod_ref[0, 0]
    tn = o_ref.shape[0] // FFN_SPLIT
    for t0 in range(0, o_ref.shape[0], tn):
        ts = slice(t0, t0 + tn)
        x = x_refs[0][ts, :]
        if n_first is not None:
            x = jnp.where(pl.program_id(0) < n_first, x, x_refs[1][ts, :])
        h = _modulate(x, ng_ref[0, gi:gi + 1], md[k0:k0 + 1], md[k0 + 1:k0 + 2]).astype(BF16)
        for lo, hi in chunks:
            a = _dot(h, wup_ref[0, :, lo:hi])
            b = _dot(h, wup_ref[0, :, dff + lo:dff + hi])
            u_scr[ts, lo:hi] = (a * _sigmoid(a) * b).astype(BF16)
        y = _dot(u_scr[ts, :], wdn_ref[0])
        o_ref[ts, :] = x + (0.5 * md[k0 + 2:k0 + 3]) * y


def _ffn_call(xs, mod4, norm_g, wup, wdn, *, layer, k0, gi, n_tiles, tm, grp):
    xs = xs if isinstance(xs, tuple) else (xs,)
    d = xs[0].shape[1]
    dff = wdn.shape[1]
    width = max(MXU_COLS, FFN_CHUNK_ELEMS // (tm // FFN_SPLIT) // MXU_COLS * MXU_COLS)
    chunks = tuple((lo, min(lo + width, dff)) for lo in range(0, dff, width))
    n_first = xs[0].shape[0] // tm if len(xs) == 2 else None
    kern = functools.partial(_ffn_kernel, k0=k0, gi=gi, dff=dff, chunks=chunks, n_first=n_first)
    if n_first is None:
        x_specs = [pl.BlockSpec((tm, d), lambda i: (i, 0))]
    else:
        x_specs = [pl.BlockSpec((tm, d), lambda i: (jnp.minimum(i, n_first - 1), 0)),
                   pl.BlockSpec((tm, d), lambda i: (jnp.maximum(i - n_first, 0), 0))]
    return pl.pallas_call(
        kern,
        grid=(n_tiles,),
        in_specs=x_specs + [
            pl.BlockSpec((1, 1, N_MOD, d), lambda i: (layer, grp(i), 0, 0)),
            pl.BlockSpec((1, 3, d), lambda i: (layer, 0, 0)),
            _resident((1, d, 2 * dff), lambda i: (layer, 0, 0)),
            _resident((1, dff, d), lambda i: (layer, 0, 0)),
        ],
        out_specs=pl.BlockSpec((tm, d), lambda i: (i, 0)),
        out_shape=jax.ShapeDtypeStruct((n_tiles * tm, d), F32),
        scratch_shapes=[pltpu.VMEM((tm, dff), BF16)],
        compiler_params=_params(("arbitrary",)),
        name="ffn",
    )(*xs, mod4, norm_g, wup, wdn)


_T_CQ = (0, 384)
_T_CKV = (384, 640)
_T_DQ = (640, 1152)
_T_DV = (1152, 1664)
_N_CKV = (0, 256)
_N_KR = (256, 384)
_N_DK = (384, 896)
_N_CC = (896, 1408)
_N_CX = (1408, 1920)


def _partner_lanes(x, q4):
    first = (lax.broadcasted_iota(jnp.int32, x.shape, 1) & q4) == 0
    return jnp.where(first, pltpu.roll(x, LANES - q4, 1), pltpu.roll(x, q4, 1))


INPROJ_SPLIT = 2


def _tile_lanes(g, n):
    return jnp.concatenate([g] * (n // LANES), axis=1)


def _rms_t(x, g, n):
    ss = jnp.sum(x * x, axis=0, keepdims=True)
    return x * lax.rsqrt(ss * (1.0 / n) + EPS) * g


def _swap_row_blocks(x, width):
    parts = []
    for r0 in range(0, x.shape[0], 2 * width):
        parts += [x[r0 + width:r0 + 2 * width], x[r0:r0 + width]]
    return jnp.concatenate(parts, axis=0)


def _inproj_kernel(x_ref, mod_ref, ng_ref, wt_ref, wn_ref, gcqt_ref, wuqt_ref, gckvt_ref, wvt_ref,
                   gckv_ref, wk_ref, gqmt_ref, gkm_ref, gqdt_ref, gkd_ref,
                   cm_ref, sgm_ref, cd_ref, sgd_ref, cmt_ref, sgmt_ref, cdt_ref, sgdt_ref,
                   qmt_ref, km_ref, vmt_ref, dqt_ref, dk_ref, dvt_ref, u_ref, wt_scr):
    @pl.when(pl.program_id(0) == 0)
    def _():
        wt_scr[...] = wt_ref[0].T

    md = mod_ref[0, 0]
    tn = x_ref.shape[0] // INPROJ_SPLIT
    for t0 in range(0, x_ref.shape[0], tn):
        ts = slice(t0, t0 + tn)
        hf = _modulate(x_ref[ts, :], ng_ref[0, 1:2], md[3:4], md[4:5])
        h = hf.astype(BF16)
        ht = hf.T.astype(BF16)

        pt = _dot(wt_scr[...], ht)
        pn = _dot(h, wn_ref[0])

        def rows(seg):
            return pt[seg[0]:seg[1]]

        def cols(seg):
            return pn[:, seg[0]:seg[1]]

        cqn_t = _rms_t(rows(_T_CQ), _tile_lanes(gcqt_ref[0], tn), _T_CQ[1] - _T_CQ[0]).astype(BF16)
        q_raw_t = _dot(wuqt_ref[0], cqn_t)
        ckvn_t = _rms_t(rows(_T_CKV), _tile_lanes(gckvt_ref[0], tn), _T_CKV[1] - _T_CKV[0]).astype(BF16)
        vmt_ref[:, ts] = _dot(wvt_ref[0], ckvn_t).astype(BF16)
        dvt_ref[:, ts] = rows(_T_DV).astype(BF16)

        cmt, sgmt = cmt_ref[:, ts], sgmt_ref[:, ts]
        gqmt = _tile_lanes(gqmt_ref[0], tn)
        q_scale = MLA_QK ** -0.5 * LOG2E
        for hd in range(MLA_HEADS):
            blk = slice(hd * LANES, (hd + 1) * LANES)
            qn = _rms_t(q_raw_t[blk], gqmt, MLA_QK)
            qr = qn * cmt + _swap_row_blocks(qn, MLA_ROPE // 4) * sgmt
            qmt_ref[blk, ts] = (qr * q_scale).astype(BF16)

        cdt, sgdt = cdt_ref[:, ts], sgdt_ref[:, ts]
        gqdt = _tile_lanes(gqdt_ref[0], tn)
        d_scale = DIFF_DIM ** -0.5 * LOG2E
        dq_t = rows(_T_DQ)
        for hd in range(DIFF_HEADS):
            halves = []
            for comp in range(2):
                r0 = hd * LANES + comp * DIFF_DIM
                xc = dq_t[r0:r0 + DIFF_DIM]
                ss = jnp.sum(xc * xc, axis=0, keepdims=True)
                halves.append(xc * lax.rsqrt(ss * (1.0 / DIFF_DIM) + EPS))
            xn = jnp.concatenate(halves, axis=0) * gqdt
            xr = xn * cdt + _swap_row_blocks(xn, DIFF_DIM // 4) * sgdt
            dqt_ref[hd * LANES:(hd + 1) * LANES, ts] = (xr * d_scale).astype(BF16)

        ckvn = _rms(cols(_N_CKV), gckv_ref[0], _N_CKV[1] - _N_CKV[0]).astype(BF16)
        k_raw = _dot(ckvn, wk_ref[0])
        krope = cols(_N_KR)
        gc_m = gkm_ref[0, 0:1] * cm_ref[ts, :]
        rot_m = _partner_lanes(krope, MLA_ROPE // 4) * (gkm_ref[0, 1:2] * sgm_ref[ts, :])
        for hd in range(MLA_HEADS):
            blk = slice(hd * LANES, (hd + 1) * LANES)
            kf = k_raw[:, blk] + krope
            r = lax.rsqrt(jnp.sum(kf * kf, axis=-1, keepdims=True) * (1.0 / MLA_QK) + EPS)
            km_ref[ts, blk] = ((kf * gc_m + rot_m) * r).astype(BF16)

        gc_d = gkd_ref[0, 0:1] * cd_ref[ts, :]
        gs_d = gkd_ref[0, 1:2] * sgd_ref[ts, :]
        lo = lax.broadcasted_iota(jnp.int32, (tn, LANES), 1) < DIFF_DIM
        dk = cols(_N_DK)
        for hd in range(DIFF_HEADS):
            blk = slice(hd * LANES, (hd + 1) * LANES)
            xh = dk[:, blk]
            sq = xh * xh
            s_lo = jnp.sum(jnp.where(lo, sq, 0.0), axis=-1, keepdims=True)
            s_hi = jnp.sum(jnp.where(lo, 0.0, sq), axis=-1, keepdims=True)
            r = lax.rsqrt(jnp.where(lo, s_lo, s_hi) * (1.0 / DIFF_DIM) + EPS)
            dk_ref[ts, blk] = ((xh * gc_d + _partner_lanes(xh, DIFF_DIM // 4) * gs_d) * r).astype(BF16)

        u_ref[ts, :] = cols(_N_CC) * cols(_N_CX)


def _inproj_call(xs, mod4, norm_g, wt, wn, gcqt, wuqt, gckvt, wvt, gckv, wk, gqmt, gkm, gqdt, gkd, tabs, tabs_t,
                 *, layer, n_tiles, tm, grp, tab_blk):
    r, d = xs.shape

    def vec(n):
        return pl.BlockSpec((1, 1, n), lambda i: (layer, 0, 0))

    def res(a):
        return _resident((1,) + a.shape[1:], lambda i: (layer, 0, 0))

    def tab():
        return pl.BlockSpec((tm, LANES), lambda i: (tab_blk(i), 0))

    def tab_t():
        return pl.BlockSpec((LANES, tm), lambda i: (0, tab_blk(i)))

    def rows(n):
        return pl.BlockSpec((tm, n), lambda i: (i, 0))

    hm, hd = MLA_HEADS * LANES, DIFF_HEADS * LANES
    out_shapes = [
        jax.ShapeDtypeStruct((hm, r), BF16),
        jax.ShapeDtypeStruct((r, hm), BF16),
        jax.ShapeDtypeStruct((BRANCH_WIDTH, r), BF16),
        jax.ShapeDtypeStruct((hd, r), BF16),
        jax.ShapeDtypeStruct((r, hd), BF16),
        jax.ShapeDtypeStruct((hd, r), BF16),
        jax.ShapeDtypeStruct((r, BRANCH_WIDTH), F32),
    ]
    return pl.pallas_call(
        _inproj_kernel,
        grid=(n_tiles,),
        in_specs=[
            rows(d),
            pl.BlockSpec((1, 1, N_MOD, d), lambda i: (layer, grp(i), 0, 0)),
            pl.BlockSpec((1, 3, d), lambda i: (layer, 0, 0)),
            res(wt), res(wn), res(gcqt), res(wuqt), res(gckvt), res(wvt),
            vec(gckv.shape[2]), res(wk), res(gqmt), res(gkm), res(gqdt), res(gkd),
            tab(), tab(), tab(), tab(),
            tab_t(), tab_t(), tab_t(), tab_t(),
        ],
        out_specs=[rows(s.shape[1]) if s.shape[0] == r else pl.BlockSpec((s.shape[0], tm), lambda i: (0, i))
                   for s in out_shapes],
        out_shape=out_shapes,
        scratch_shapes=[pltpu.VMEM((wt.shape[2], wt.shape[1]), BF16)],
        compiler_params=_params(("arbitrary",)),
        name="inproj",
    )(xs, mod4, norm_g, wt, wn, gcqt, wuqt, gckvt, wvt, gckv, wk, gqmt, gkm, gqdt, gkd, *tabs, *tabs_t)


KEY_CHUNK = 256
LOGIT_SLOTS = 3


def _col_stat(x, op):
    t, n = x.shape
    r = 64 if t % 64 == 0 else SUBLANES
    return op(op(x.reshape(t // r, r, n), axis=0), axis=0, keepdims=True)


def _flash_heads(n_heads, get_qt, kblk, vrows, k_refs, vt_refs, s_scr, finish):
    depth = s_scr.shape[0]
    chunks = [(k_ref, vt_ref, c0, min(KEY_CHUNK, k_ref.shape[0])) for k_ref, vt_ref in zip(k_refs, vt_refs)
              for c0 in range(0, k_ref.shape[0], min(KEY_CHUNK, k_ref.shape[0]))]
    steps = [(h, ci) for h in range(n_heads) for ci in range(len(chunks))]

    def logits(n):
        h, ci = steps[n]
        k_ref, _, c0, ck = chunks[ci]
        s_scr[n % depth, 0:ck] = _dot(k_ref[c0:c0 + ck, kblk(h)], get_qt(h))

    for n in range(min(depth - 1, len(steps))):
        logits(n)
    m = acc = None
    for n, (h, ci) in enumerate(steps):
        if n + depth - 1 < len(steps):
            logits(n + depth - 1)
        _, vt_ref, c0, ck = chunks[ci]
        s = s_scr[n % depth, 0:ck]
        cm = _col_stat(s, jnp.max)
        m_new = cm if ci == 0 else jnp.maximum(m, cm)
        p = jnp.exp2(s - m_new).astype(BF16)
        vt1 = jnp.concatenate([vt_ref[vrows(h), c0:c0 + ck], jnp.ones((BF16_ROWS, ck), BF16)], axis=0)
        pv = _dot(vt1, p)
        acc = pv if ci == 0 else jnp.exp2(m - m_new) * acc + pv
        m = m_new
        if ci == len(chunks) - 1:
            dv = acc.shape[0] - BF16_ROWS
            finish(h, acc[:dv] * (1.0 / acc[dv:dv + 1]))


def _mla_attn_kernel(*refs, n_seg):
    qt_ref = refs[0]
    k_refs = refs[1:1 + n_seg]
    vt_refs = refs[1 + n_seg:1 + 2 * n_seg]
    o_ref, s_scr = refs[-2:]
    held = []

    def finish(h, ot):
        held.append(ot)
        if h % 2 == 1:
            pair_t = jnp.concatenate(held, axis=0)
            o_ref[:, (h // 2) * LANES:(h // 2 + 1) * LANES] = pair_t.T.astype(BF16)
            held.clear()

    _flash_heads(MLA_HEADS, lambda h: qt_ref[h * LANES:(h + 1) * LANES, :],
                 lambda h: slice(h * LANES, (h + 1) * LANES), lambda h: slice(h * MLA_V, (h + 1) * MLA_V),
                 k_refs, vt_refs, s_scr, finish)


def _diff_attn_kernel(*refs, n_seg, lam_init):
    qt_ref = refs[0]
    k_refs = refs[1:1 + n_seg]
    vt_refs = refs[1 + n_seg:1 + 2 * n_seg]
    lam_ref, gs_ref = refs[1 + 2 * n_seg:3 + 2 * n_seg]
    o_ref, s_scr = refs[-2:]
    bq = qt_ref.shape[1]
    lv = lam_ref[0]
    lam = (jnp.exp(jnp.sum(lv[0:1] * lv[1:2], axis=-1, keepdims=True))
           - jnp.exp(jnp.sum(lv[2:3] * lv[3:4], axis=-1, keepdims=True)) + lam_init)
    gs = gs_ref[0]
    lo = lax.broadcasted_iota(jnp.int32, (LANES, bq), 0) < DIFF_DIM
    blk = lambda h: slice(h * LANES, (h + 1) * LANES)

    def get_qt(h):
        qt = qt_ref[blk(h), :]
        zero = jnp.zeros_like(qt)
        return jnp.concatenate([jnp.where(lo, qt, zero), jnp.where(lo, zero, qt)], axis=1)

    def finish(h, oc):
        o = (oc[:, :bq] - lam * oc[:, bq:]).T
        o_ref[:, blk(h)] = (_rms(o, gs, DIFF_V) * (1.0 - lam_init)).astype(BF16)

    _flash_heads(DIFF_HEADS, get_qt, blk, blk, k_refs, vt_refs, s_scr, finish)


def _attn_call(kern, qt, k, vt, extra, *, width_qk, width_v, q_cols, batch, seq, ctx, r_lat, bq,
               latent, name):
    extra_specs = [pl.BlockSpec((1,) + a.shape[1:], functools.partial(lambda l, b, i: (l, 0, 0), lyr))
                   for a, lyr in extra]
    extra_args = [a for a, _ in extra]
    ctx_blk0 = r_lat // ctx
    if latent:
        nq = seq // bq
        grid = (batch, nq)
        q_spec = pl.BlockSpec((width_qk, bq), lambda b, i: (0, b * nq + i))
        k_specs = [pl.BlockSpec((ctx, width_qk), lambda b, i: (ctx_blk0 + b, 0)),
                   pl.BlockSpec((seq, width_qk), lambda b, i: (b, 0))]
        v_specs = [pl.BlockSpec((width_v, ctx), lambda b, i: (0, ctx_blk0 + b)),
                   pl.BlockSpec((width_v, seq), lambda b, i: (0, b))]
        o_spec = pl.BlockSpec((bq, width_v), lambda b, i: (b * nq + i, 0))
        k_args, v_args, o_rows = [k, k], [vt, vt], r_lat
    else:
        grid = (batch, 1)
        q_spec = pl.BlockSpec((width_qk, ctx), lambda b, i: (0, ctx_blk0 + b))
        k_specs = [pl.BlockSpec((ctx, width_qk), lambda b, i: (ctx_blk0 + b, 0))]
        v_specs = [pl.BlockSpec((width_v, ctx), lambda b, i: (0, ctx_blk0 + b))]
        o_spec = pl.BlockSpec((ctx, width_v), lambda b, i: (b, 0))
        k_args, v_args, o_rows = [k], [vt], batch * ctx
    return pl.pallas_call(
        functools.partial(kern, n_seg=len(k_args)),
        grid=grid,
        in_specs=[q_spec] + k_specs + v_specs + extra_specs,
        out_specs=o_spec,
        out_shape=jax.ShapeDtypeStruct((o_rows, width_v), BF16),
        scratch_shapes=[pltpu.VMEM((LOGIT_SLOTS, KEY_CHUNK, q_cols * q_spec.block_shape[1]), F32)],
        compiler_params=_params(("arbitrary", "arbitrary")),
        name=name,
    )(qt, *k_args, *v_args, *extra_args)


def _merge_kernel(x_ref, mod_ref, ng_ref, wgc_ref, oml_ref, omc_ref, odl_ref, odc_ref, u_ref, up_ref, un_ref,
                  cw_ref, wbr_ref, wo_ref, o_ref, *, n_lat_tiles, seq, ctx):
    i = pl.program_id(0)
    is_lat = i < n_lat_tiles
    tm, d = x_ref.shape
    x = x_ref[...]
    md = mod_ref[0, 0]
    h = _modulate(x, ng_ref[0, 1:2], md[3:4], md[4:5]).astype(BF16)

    u = u_ref[...]
    row = lax.broadcasted_iota(jnp.int32, (tm, 1), 0)
    seq_len = jnp.where(is_lat, seq, ctx)
    pos = (i * tm + row) & (seq_len - 1)
    u_dn = jnp.where(row == 0, up_ref[SUBLANES - 1:SUBLANES, :], pltpu.roll(u, 1, 0))
    u_dn = jnp.where(pos == 0, 0.0, u_dn)
    u_up = jnp.where(row == tm - 1, un_ref[0:1, :], pltpu.roll(u, tm - 1, 0))
    u_up = jnp.where(pos == seq_len - 1, 0.0, u_up)
    cw = cw_ref[0]
    y = cw[0:1] * u_dn + cw[1:2] * u + cw[2:3] * u_up
    conv = (_dot(h, wgc_ref[0, :, 3 * d:]) * y).astype(BF16)

    branches = (jnp.where(is_lat, oml_ref[...], omc_ref[...]), jnp.where(is_lat, odl_ref[...], odc_ref[...]), conv)
    m = None
    for n, yb in enumerate(branches):
        g = _sigmoid(_dot(h, wgc_ref[0, :, n * d:(n + 1) * d]))
        t = g * _dot(yb, wbr_ref[0, n])
        m = t if m is None else m + t
    o_ref[...] = x + md[5:6] * _dot(m.astype(BF16), wo_ref[0])


def _merge_call(xs, mod4, norm_g, wgc, om_lat, om_ctx, od_lat, od_ctx, u, conv_w, wbr, wo, *, layer, n_tiles,
                tm, grp, n_lat_tiles, seq, ctx):
    r, d = xs.shape
    hb = tm // SUBLANES
    last8 = u.shape[0] // SUBLANES - 1

    def rows(n):
        return pl.BlockSpec((tm, n), lambda i: (i, 0))

    lat_rows = pl.BlockSpec((tm, BRANCH_WIDTH), lambda i: (jnp.minimum(i, n_lat_tiles - 1), 0))
    ctx_rows = pl.BlockSpec((tm, BRANCH_WIDTH), lambda i: (jnp.maximum(i - n_lat_tiles, 0), 0))

    kern = functools.partial(_merge_kernel, n_lat_tiles=n_lat_tiles, seq=seq, ctx=ctx)
    return pl.pallas_call(
        kern,
        grid=(n_tiles,),
        in_specs=[
            rows(d),
            pl.BlockSpec((1, 1, N_MOD, d), lambda i: (layer, grp(i), 0, 0)),
            pl.BlockSpec((1, 3, d), lambda i: (layer, 0, 0)),
            _resident((1,) + wgc.shape[1:], lambda i: (layer, 0, 0)),
            lat_rows, ctx_rows, lat_rows, ctx_rows, rows(BRANCH_WIDTH),
            pl.BlockSpec((SUBLANES, BRANCH_WIDTH), lambda i: (jnp.maximum(i * hb - 1, 0), 0)),
            pl.BlockSpec((SUBLANES, BRANCH_WIDTH), lambda i: (jnp.minimum((i + 1) * hb, last8), 0)),
            pl.BlockSpec((1,) + conv_w.shape[1:], lambda i: (layer, 0, 0)),
            _resident((1,) + wbr.shape[1:], lambda i: (layer, 0, 0, 0)),
            _resident((1, d, d), lambda i: (layer, 0, 0)),
        ],
        out_specs=rows(d),
        out_shape=jax.ShapeDtypeStruct((n_tiles * tm, d), F32),
        compiler_params=_params(("arbitrary",)),
        name="merge",
    )(xs, mod4, norm_g, wgc, om_lat, om_ctx, od_lat, od_ctx, u, u, u, conv_w, wbr, wo)


def _rope_tables(seq, tm):
    t = jnp.arange(seq)
    row = (t // GRID_W).astype(F32)[:, None]
    col = (t % GRID_W).astype(F32)[:, None]

    lane = jnp.arange(LANES)

    def axis_tables(rot_dim, rel, live):
        q4 = rot_dim // 4
        inv = ROPE_THETA ** (-jnp.arange(q4, dtype=F32) / q4)
        ang = jnp.where((rel // (2 * q4)) == 0, row, col) * inv[rel % q4][None, :]
        first = ((rel // q4) % 2) == 0
        c = jnp.where(live, jnp.cos(ang), 1.0)
        s = jnp.where(live, jnp.where(first, -jnp.sin(ang), jnp.sin(ang)), 0.0)
        ident = [jnp.ones((tm, LANES), F32), jnp.zeros((tm, LANES), F32)]
        return [jnp.concatenate([a, b], axis=0) for a, b in zip((c, s), ident)]

    mla_live = (lane >= MLA_NOPE) & (lane < MLA_QK)
    mla = axis_tables(MLA_ROPE, jnp.where(mla_live, lane - MLA_NOPE, 0), mla_live)
    diff = axis_tables(DIFF_DIM, lane % DIFF_DIM, lane >= 0)
    return mla + diff


def _pad_last(a, n):
    return jnp.pad(a, [(0, 0)] * (a.ndim - 1) + [(0, n - a.shape[-1])])


def kernel(x, c, ctx, c_ctx, w_mod, b_mod, norm_g, ffn1_up, ffn1_down, ffn2_up, ffn2_down,
           w_in, g_cq, w_uq, g_ckv, w_ukv, g_q_mla, g_k_mla, g_q_diff, g_k_diff, lam,
           g_subln, conv_w, w_br, w_o):
    batch, seq, d = x.shape
    n_ctx = ctx.shape[1]
    depth = w_mod.shape[0]
    r_lat, r_ctx = batch * seq, batch * n_ctx
    tm, tm_ffn = 512, 1024
    bq_mla, bq_diff = 512, 256
    assert seq % tm == 0 and r_ctx % tm == 0 and seq % bq_mla == 0 and r_lat % n_ctx == 0
    assert seq % tm_ffn == 0 and r_ctx % tm_ffn == 0
    assert n_ctx % min(KEY_CHUNK, n_ctx) == 0 and seq % KEY_CHUNK == 0
    assert seq & (seq - 1) == 0 and n_ctx & (n_ctx - 1) == 0 and seq % GRID_W == 0
    assert d == 1024 and w_br.shape[2] == BRANCH_WIDTH
    n_lat_tiles, n_all_tiles = r_lat // tm, (r_lat + r_ctx) // tm
    tiles_per_batch = seq // tm

    def grp_of(tile):
        return lambda i: jnp.where(i < r_lat // tile, i // (seq // tile), batch)

    grp, grp_ffn = grp_of(tm), grp_of(tm_ffn)

    def tab_blk(i):
        return jnp.where(i < n_lat_tiles, i % tiles_per_batch, tiles_per_batch)

    rank_q, rank_kv = g_cq.shape[1], g_ckv.shape[1]
    assert (rank_q, rank_kv) == (_T_CQ[1] - _T_CQ[0], _T_CKV[1] - _T_CKV[0])
    o = [0]
    for n in (rank_q, rank_kv, MLA_ROPE, 512, 512, 512, 512, 512, 512, 3 * d):
        o.append(o[-1] + n)
    seg = lambda k: w_in[:, :, o[k]:o[k + 1]]
    zeros = lambda n: jnp.zeros((depth, d, n), w_in.dtype)
    swap = lambda a: jnp.swapaxes(a, 1, 2)
    lanes = lambda g: jnp.broadcast_to(g[:, :, None], g.shape + (LANES,))
    wt = jnp.concatenate([seg(0), seg(1), seg(3), seg(5)], axis=-1).astype(BF16)
    def partner(a, rot_dim):
        q4 = rot_dim // 4
        g = a.reshape(a.shape[:-1] + (a.shape[-1] // rot_dim, 2, 2, q4))
        return jnp.flip(g, axis=-2).reshape(a.shape)

    def place_rope(a):
        return jnp.concatenate([jnp.zeros(a.shape[:-1] + (MLA_NOPE,), a.dtype), a,
                                jnp.zeros(a.shape[:-1] + (LANES - MLA_QK,), a.dtype)], axis=-1)

    wn = jnp.concatenate([seg(1), place_rope(seg(2)), seg(4), seg(7), seg(8)], axis=-1).astype(BF16)
    wgc = jnp.concatenate([seg(9), seg(6)], axis=-1).astype(BF16)
    wuq = _pad_last(w_uq.reshape(depth, rank_q, MLA_HEADS, MLA_QK), LANES)
    wuqt = swap(wuq.reshape(depth, rank_q, MLA_HEADS * LANES)).astype(BF16)
    wkv4 = w_ukv.reshape(depth, rank_kv, MLA_HEADS, MLA_NOPE + MLA_V)
    wk = _pad_last(wkv4[..., :MLA_NOPE], LANES).reshape(depth, rank_kv, MLA_HEADS * LANES).astype(BF16)
    wvt = swap(wkv4[..., MLA_NOPE:].reshape(depth, rank_kv, MLA_HEADS * MLA_V)).astype(BF16)
    ffn_w = [(ffn1_up.astype(BF16), ffn1_down.astype(BF16)), (ffn2_up.astype(BF16), ffn2_down.astype(BF16))]
    wbr = w_br.astype(BF16)
    wo = w_o.astype(BF16)
    gcqt = lanes(g_cq)
    gckvt = lanes(g_ckv)
    gckv = g_ckv[:, None, :]
    gqmt = lanes(_pad_last(g_q_mla, LANES))
    gkm = jnp.stack([_pad_last(g_k_mla, LANES),
                     place_rope(partner(g_k_mla[:, MLA_NOPE:], MLA_ROPE))], axis=1)
    gqdt = lanes(jnp.tile(g_q_diff, (1, 2)))
    gkd = jnp.stack([jnp.tile(g_k_diff, (1, 2)), jnp.tile(partner(g_k_diff, DIFF_DIM), (1, 2))], axis=1)
    gsub = g_subln[:, None, :]
    tabs = _rope_tables(seq, tm)
    tabs_t = [t.T for t in tabs]

    cond = jnp.concatenate([c, c_ctx[None, :]], axis=0)
    rows_pad = (batch + 1 + SUBLANES - 1) // SUBLANES * SUBLANES
    cond = jnp.pad(cond, ((0, rows_pad - batch - 1), (0, 0)))
    mod4 = _mod_call(cond, w_mod, b_mod).reshape(depth, rows_pad, N_MOD, d)

    xs = (x.reshape(r_lat, d), ctx.reshape(r_ctx, d))
    attn_kw = dict(batch=batch, seq=seq, ctx=n_ctx, r_lat=r_lat)
    for l in range(depth):
        last = l == depth - 1
        lam_init = 0.8 - 0.6 * math.exp(-0.3 * l)
        tm1, grp1 = (tm, grp) if l == 0 else (tm_ffn, grp_ffn)
        xs = _ffn_call(xs, mod4, norm_g, *ffn_w[0], layer=l, k0=0, gi=0,
                       n_tiles=n_all_tiles * tm // tm1, tm=tm1, grp=grp1)
        qmt, km, vmt, dqt, dk, dvt, u = _inproj_call(
            xs, mod4, norm_g, wt, wn, gcqt, wuqt, gckvt, wvt, gckv, wk, gqmt, gkm, gqdt, gkd, tabs, tabs_t,
            layer=l, n_tiles=n_all_tiles, tm=tm, grp=grp, tab_blk=tab_blk)
        diff_kern = functools.partial(_diff_attn_kernel, lam_init=lam_init)
        diff_extra = [(lam, l), (gsub, l)]
        mla_kw = dict(width_qk=MLA_HEADS * LANES, width_v=BRANCH_WIDTH, q_cols=1, bq=bq_mla, **attn_kw)
        diff_kw = dict(width_qk=DIFF_HEADS * LANES, width_v=BRANCH_WIDTH, q_cols=2, bq=bq_diff, **attn_kw)
        om = _attn_call(_mla_attn_kernel, qmt, km, vmt, [], latent=True, name="mla_lat", **mla_kw)
        od = _attn_call(diff_kern, dqt, dk, dvt, diff_extra, latent=True, name="diff_lat", **diff_kw)
        n_tiles, om_ctx, od_ctx = n_lat_tiles, om, od
        if not last:
            om_ctx = _attn_call(_mla_attn_kernel, qmt, km, vmt, [], latent=False, name="mla_ctx", **mla_kw)
            od_ctx = _attn_call(diff_kern, dqt, dk, dvt, diff_extra, latent=False, name="diff_ctx", **diff_kw)
            n_tiles = n_all_tiles
        xs = _merge_call(xs, mod4, norm_g, wgc, om, om_ctx, od, od_ctx, u, conv_w, wbr, wo, layer=l,
                         n_tiles=n_tiles, tm=tm, grp=grp, n_lat_tiles=n_lat_tiles, seq=seq, ctx=n_ctx)
        xs = _ffn_call(xs, mod4, norm_g, *ffn_w[1], layer=l, k0=6, gi=2,
                       n_tiles=n_tiles * tm // tm_ffn, tm=tm_ffn, grp=grp_ffn)
    return xs.reshape(batch, seq, d)
```

```python
import functools
import math

import jax
import jax.numpy as jnp
from jax import lax
from jax.experimental import pallas as pl
from jax.experimental.pallas import tpu as pltpu

GRID_W = 64
N_MOD = 9
MLA_HEADS = 8
MLA_NOPE = 64
MLA_ROPE = 32
MLA_QK = MLA_NOPE + MLA_ROPE
MLA_V = 64
DIFF_HEADS = 4
DIFF_DIM = 64
DIFF_V = 2 * DIFF_DIM
BRANCH_WIDTH = 512
ROPE_THETA = 10000.0
EPS = 1e-6
LOG2E = math.log2(math.e)

LANES = 128
SUBLANES = 8
BF16_ROWS = 16
MXU_COLS = 256
FFN_CHUNK_ELEMS = 768 * 1024
FFN_SPLIT = 2
VMEM_LIMIT_BYTES = 56 * 1024 * 1024
BF16 = jnp.bfloat16
F32 = jnp.float32


def _dot(a, b):
    return jnp.dot(a, b, preferred_element_type=F32)


def _rms(x, g, n):
    ss = jnp.sum(x * x, axis=-1, keepdims=True)
    return x * lax.rsqrt(ss * (1.0 / n) + EPS) * g


def _modulate(x, g, shift, scale):
    return _rms(x, g, x.shape[-1]) * (1.0 + scale) + shift


def _sigmoid(x):
    return 1.0 / (1.0 + jnp.exp(-x))


def _params(sem):
    return pltpu.CompilerParams(dimension_semantics=sem, vmem_limit_bytes=VMEM_LIMIT_BYTES)


def _resident(block_shape, index_map):
    return pl.BlockSpec(block_shape, index_map, pipeline_mode=pl.Buffered(1))


def _mod_kernel(cond_ref, w_ref, b_ref, o_ref):
    c = cond_ref[...]
    a = (c * _sigmoid(c)).astype(BF16)
    o_ref[0] = _dot(a, w_ref[0].astype(BF16)) + b_ref[0]


def _mod_call(cond, w_mod, b_mod):
    depth, d, nd = w_mod.shape
    rows = cond.shape[0]
    tn = d
    return pl.pallas_call(
        _mod_kernel,
        grid=(depth, nd // tn),
        in_specs=[
            pl.BlockSpec((rows, d), lambda l, j: (0, 0)),
            pl.BlockSpec((1, d, tn), lambda l, j: (l, 0, j)),
            pl.BlockSpec((1, 1, tn), lambda l, j: (l, 0, j)),
        ],
        out_specs=pl.BlockSpec((1, rows, tn), lambda l, j: (l, 0, j)),
        out_shape=jax.ShapeDtypeStruct((depth, rows, nd), F32),
        compiler_params=_params(("arbitrary", "arbitrary")),
        name="mod",
    )(cond, w_mod, b_mod.reshape(depth, 1, nd))


def _ffn_kernel(*refs, k0, gi, dff, chunks, n_first):
    x_refs, (mod_ref, ng_ref, wup_ref, wdn_ref, o_ref, u_scr) = refs[:-6], refs[-6:]
    md = mod_ref[0, 0]
    tn = o_ref.shape[0] // FFN_SPLIT
    for t0 in range(0, o_ref.shape[0], tn):
        ts = slice(t0, t0 + tn)
        x = x_refs[0][ts, :]
        if n_first is not None:
            x = jnp.where(pl.program_id(0) < n_first, x, x_refs[1][ts, :])
        h = _modulate(x, ng_ref[0, gi:gi + 1], md[k0:k0 + 1], md[k0 + 1:k0 + 2]).astype(BF16)
        for lo, hi in chunks:
            a = _dot(h, wup_ref[0, :, lo:hi])
            b = _dot(h, wup_ref[0, :, dff + lo:dff + hi])
            u_scr[ts, lo:hi] = (a * _sigmoid(a) * b).astype(BF16)
        y = _dot(u_scr[ts, :], wdn_ref[0])
        o_ref[ts, :] = x + (0.5 * md[k0 + 2:k0 + 3]) * y


def _ffn_call(xs, mod4, norm_g, wup, wdn, *, layer, k0, gi, n_tiles, tm, grp):
    xs = xs if isinstance(xs, tuple) else (xs,)
    d = xs[0].shape[1]
    dff = wdn.shape[1]
    width = max(MXU_COLS, FFN_CHUNK_ELEMS // (tm // FFN_SPLIT) // MXU_COLS * MXU_COLS)
    chunks = tuple((lo, min(lo + width, dff)) for lo in range(0, dff, width))
    n_first = xs[0].shape[0] // tm if len(xs) == 2 else None
    kern = functools.partial(_ffn_kernel, k0=k0, gi=gi, dff=dff, chunks=chunks, n_first=n_first)
    if n_first is None:
        x_specs = [pl.BlockSpec((tm, d), lambda i: (i, 0))]
    else:
        x_specs = [pl.BlockSpec((tm, d), lambda i: (jnp.minimum(i, n_first - 1), 0)),
                   pl.BlockSpec((tm, d), lambda i: (jnp.maximum(i - n_first, 0), 0))]
    return pl.pallas_call(
        kern,
        grid=(n_tiles,),
        in_specs=x_specs + [
            pl.BlockSpec((1, 1, N_MOD, d), lambda i: (layer, grp(i), 0, 0)),
            pl.BlockSpec((1, 3, d), lambda i: (layer, 0, 0)),
            _resident((1, d, 2 * dff), lambda i: (layer, 0, 0)),
            _resident((1, dff, d), lambda i: (layer, 0, 0)),
        ],
        out_specs=pl.BlockSpec((tm, d), lambda i: (i, 0)),
        out_shape=jax.ShapeDtypeStruct((n_tiles * tm, d), F32),
        scratch_shapes=[pltpu.VMEM((tm, dff), BF16)],
        compiler_params=_params(("arbitrary",)),
        name="ffn",
    )(*xs, mod4, norm_g, wup, wdn)


_T_CQ = (0, 384)
_T_CKV = (384, 640)
_T_DQ = (640, 1152)
_T_DV = (1152, 1664)
_N_CKV = (0, 256)
_N_KR = (256, 384)
_N_DK = (384, 896)
_N_CC = (896, 1408)
_N_CX = (1408, 1920)


def _partner_lanes(x, q4):
    first = (lax.broadcasted_iota(jnp.int32, x.shape, 1) & q4) == 0
    return jnp.where(first, pltpu.roll(x, LANES - q4, 1), pltpu.roll(x, q4, 1))


INPROJ_SPLIT = 2


def _tile_lanes(g, n):
    return jnp.concatenate([g] * (n // LANES), axis=1)


def _rms_t(x, g, n):
    ss = jnp.sum(x * x, axis=0, keepdims=True)
    return x * lax.rsqrt(ss * (1.0 / n) + EPS) * g


def _swap_row_blocks(x, width):
    parts = []
    for r0 in range(0, x.shape[0], 2 * width):
        parts += [x[r0 + width:r0 + 2 * width], x[r0:r0 + width]]
    return jnp.concatenate(parts, axis=0)


def _inproj_kernel(x_ref, mod_ref, ng_ref, wt_ref, wn_ref, gcqt_ref, wuqt_ref, gckvt_ref, wvt_ref,
                   gckv_ref, wk_ref, gqmt_ref, gkm_ref, gqdt_ref, gkd_ref,
                   cm_ref, sgm_ref, cd_ref, sgd_ref, cmt_ref, sgmt_ref, cdt_ref, sgdt_ref,
                   qmt_ref, km_ref, vmt_ref, dqt_ref, dk_ref, dvt_ref, u_ref, wt_scr):
    @pl.when(pl.program_id(0) == 0)
    def _():
        wt_scr[...] = wt_ref[0].T

    md = mod_ref[0, 0]
    tn = x_ref.shape[0] // INPROJ_SPLIT
    for t0 in range(0, x_ref.shape[0], tn):
        ts = slice(t0, t0 + tn)
        hf = _modulate(x_ref[ts, :], ng_ref[0, 1:2], md[3:4], md[4:5])
        h = hf.astype(BF16)
        ht = hf.T.astype(BF16)

        pt = _dot(wt_scr[...], ht)
        pn = _dot(h, wn_ref[0])

        def rows(seg):
            return pt[seg[0]:seg[1]]

        def cols(seg):
            return pn[:, seg[0]:seg[1]]

        cqn_t = _rms_t(rows(_T_CQ), _tile_lanes(gcqt_ref[0], tn), _T_CQ[1] - _T_CQ[0]).astype(BF16)
        q_raw_t = _dot(wuqt_ref[0], cqn_t)
        ckvn_t = _rms_t(rows(_T_CKV), _tile_lanes(gckvt_ref[0], tn), _T_CKV[1] - _T_CKV[0]).astype(BF16)
        vmt_ref[:, ts] = _dot(wvt_ref[0], ckvn_t).astype(BF16)
        dvt_ref[:, ts] = rows(_T_DV).astype(BF16)

        cmt, sgmt = cmt_ref[:, ts], sgmt_ref[:, ts]
        gqmt = _tile_lanes(gqmt_ref[0], tn)
        q_scale = MLA_QK ** -0.5 * LOG2E
        for hd in range(MLA_HEADS):
            blk = slice(hd * LANES, (hd + 1) * LANES)
            qn = _rms_t(q_raw_t[blk], gqmt, MLA_QK)
            qr = qn * cmt + _swap_row_blocks(qn, MLA_ROPE // 4) * sgmt
            qmt_ref[blk, ts] = (qr * q_scale).astype(BF16)

        cdt, sgdt = cdt_ref[:, ts], sgdt_ref[:, ts]
        gqdt = _tile_lanes(gqdt_ref[0], tn)
        d_scale = DIFF_DIM ** -0.5 * LOG2E
        dq_t = rows(_T_DQ)
        for hd in range(DIFF_HEADS):
            halves = []
            for comp in range(2):
                r0 = hd * LANES + comp * DIFF_DIM
                xc = dq_t[r0:r0 + DIFF_DIM]
                ss = jnp.sum(xc * xc, axis=0, keepdims=True)
                halves.append(xc * lax.rsqrt(ss * (1.0 / DIFF_DIM) + EPS))
            xn = jnp.concatenate(halves, axis=0) * gqdt
            xr = xn * cdt + _swap_row_blocks(xn, DIFF_DIM // 4) * sgdt
            dqt_ref[hd * LANES:(hd + 1) * LANES, ts] = (xr * d_scale).astype(BF16)

        ckvn = _rms(cols(_N_CKV), gckv_ref[0], _N_CKV[1] - _N_CKV[0]).astype(BF16)
        k_raw = _dot(ckvn, wk_ref[0])
        krope = cols(_N_KR)
        gc_m = gkm_ref[0, 0:1] * cm_ref[ts, :]
        rot_m = _partner_lanes(krope, MLA_ROPE // 4) * (gkm_ref[0, 1:2] * sgm_ref[ts, :])
        for hd in range(MLA_HEADS):
            blk = slice(hd * LANES, (hd + 1) * LANES)
            kf = k_raw[:, blk] + krope
            r = lax.rsqrt(jnp.sum(kf * kf, axis=-1, keepdims=True) * (1.0 / MLA_QK) + EPS)
            km_ref[ts, blk] = ((kf * gc_m + rot_m) * r).astype(BF16)

        gc_d = gkd_ref[0, 0:1] * cd_ref[ts, :]
        gs_d = gkd_ref[0, 1:2] * sgd_ref[ts, :]
        lo = lax.broadcasted_iota(jnp.int32, (tn, LANES), 1) < DIFF_DIM
        dk = cols(_N_DK)
        for hd in range(DIFF_HEADS):
            blk = slice(hd * LANES, (hd + 1) * LANES)
            xh = dk[:, blk]
            sq = xh * xh
            s_lo = jnp.sum(jnp.where(lo, sq, 0.0), axis=-1, keepdims=True)
            s_hi = jnp.sum(jnp.where(lo, 0.0, sq), axis=-1, keepdims=True)
            r = lax.rsqrt(jnp.where(lo, s_lo, s_hi) * (1.0 / DIFF_DIM) + EPS)
            dk_ref[ts, blk] = ((xh * gc_d + _partner_lanes(xh, DIFF_DIM // 4) * gs_d) * r).astype(BF16)

        u_ref[ts, :] = cols(_N_CC) * cols(_N_CX)


def _inproj_call(xs, mod4, norm_g, wt, wn, gcqt, wuqt, gckvt, wvt, gckv, wk, gqmt, gkm, gqdt, gkd, tabs, tabs_t,
                 *, layer, n_tiles, tm, grp, tab_blk):
    r, d = xs.shape

    def vec(n):
        return pl.BlockSpec((1, 1, n), lambda i: (layer, 0, 0))

    def res(a):
        return _resident((1,) + a.shape[1:], lambda i: (layer, 0, 0))

    def tab():
        return pl.BlockSpec((tm, LANES), lambda i: (tab_blk(i), 0))

    def tab_t():
        return pl.BlockSpec((LANES, tm), lambda i: (0, tab_blk(i)))

    def rows(n):
        return pl.BlockSpec((tm, n), lambda i: (i, 0))

    hm, hd = MLA_HEADS * LANES, DIFF_HEADS * LANES
    out_shapes = [
        jax.ShapeDtypeStruct((hm, r), BF16),
        jax.ShapeDtypeStruct((r, hm), BF16),
        jax.ShapeDtypeStruct((BRANCH_WIDTH, r), BF16),
        jax.ShapeDtypeStruct((hd, r), BF16),
        jax.ShapeDtypeStruct((r, hd), BF16),
        jax.ShapeDtypeStruct((hd, r), BF16),
        jax.ShapeDtypeStruct((r, BRANCH_WIDTH), F32),
    ]
    return pl.pallas_call(
        _inproj_kernel,
        grid=(n_tiles,),
        in_specs=[
            rows(d),
            pl.BlockSpec((1, 1, N_MOD, d), lambda i: (layer, grp(i), 0, 0)),
            pl.BlockSpec((1, 3, d), lambda i: (layer, 0, 0)),
            res(wt), res(wn), res(gcqt), res(wuqt), res(gckvt), res(wvt),
            vec(gckv.shape[2]), res(wk), res(gqmt), res(gkm), res(gqdt), res(gkd),
            tab(), tab(), tab(), tab(),
            tab_t(), tab_t(), tab_t(), tab_t(),
        ],
        out_specs=[rows(s.shape[1]) if s.shape[0] == r else pl.BlockSpec((s.shape[0], tm), lambda i: (0, i))
                   for s in out_shapes],
        out_shape=out_shapes,
        scratch_shapes=[pltpu.VMEM((wt.shape[2], wt.shape[1]), BF16)],
        compiler_params=_params(("arbitrary",)),
        name="inproj",
    )(xs, mod4, norm_g, wt, wn, gcqt, wuqt, gckvt, wvt, gckv, wk, gqmt, gkm, gqdt, gkd, *tabs, *tabs_t)


KEY_CHUNK = 256
LOGIT_SLOTS = 3


def _col_stat(x, op):
    t, n = x.shape
    r = 64 if t % 64 == 0 else SUBLANES
    return op(op(x.reshape(t // r, r, n), axis=0), axis=0, keepdims=True)


def _flash_heads(n_heads, get_qt, kblk, vrows, k_refs, vt_refs, s_scr, finish):
    depth = s_scr.shape[0]
    chunks = [(k_ref, vt_ref, c0, min(KEY_CHUNK, k_ref.shape[0])) for k_ref, vt_ref in zip(k_refs, vt_refs)
              for c0 in range(0, k_ref.shape[0], min(KEY_CHUNK, k_ref.shape[0]))]
    steps = [(h, ci) for h in range(n_heads) for ci in range(len(chunks))]

    def logits(n):
        h, ci = steps[n]
        k_ref, _, c0, ck = chunks[ci]
        s_scr[n % depth, 0:ck] = _dot(k_ref[c0:c0 + ck, kblk(h)], get_qt(h))

    for n in range(min(depth - 1, len(steps))):
        logits(n)
    m = acc = None
    for n, (h, ci) in enumerate(steps):
        if n + depth - 1 < len(steps):
            logits(n + depth - 1)
        _, vt_ref, c0, ck = chunks[ci]
        s = s_scr[n % depth, 0:ck]
        cm = _col_stat(s, jnp.max)
        m_new = cm if ci == 0 else jnp.maximum(m, cm)
        p = jnp.exp2(s - m_new).astype(BF16)
        vt1 = jnp.concatenate([vt_ref[vrows(h), c0:c0 + ck], jnp.ones((BF16_ROWS, ck), BF16)], axis=0)
        pv = _dot(vt1, p)
        acc = pv if ci == 0 else jnp.exp2(m - m_new) * acc + pv
        m = m_new
        if ci == len(chunks) - 1:
            dv = acc.shape[0] - BF16_ROWS
            finish(h, acc[:dv] * (1.0 / acc[dv:dv + 1]))


def _mla_attn_kernel(*refs, n_seg):
    qt_ref = refs[0]
    k_refs = refs[1:1 + n_seg]
    vt_refs = refs[1 + n_seg:1 + 2 * n_seg]
    o_ref, s_scr = refs[-2:]
    held = []

    def finish(h, ot):
        held.append(ot)
        if h % 2 == 1:
            pair_t = jnp.concatenate(held, axis=0)
            o_ref[:, (h // 2) * LANES:(h // 2 + 1) * LANES] = pair_t.T.astype(BF16)
            held.clear()

    _flash_heads(MLA_HEADS, lambda h: qt_ref[h * LANES:(h + 1) * LANES, :],
                 lambda h: slice(h * LANES, (h + 1) * LANES), lambda h: slice(h * MLA_V, (h + 1) * MLA_V),
                 k_refs, vt_refs, s_scr, finish)


def _diff_attn_kernel(*refs, n_seg, lam_init):
    qt_ref = refs[0]
    k_refs = refs[1:1 + n_seg]
    vt_refs = refs[1 + n_seg:1 + 2 * n_seg]
    lam_ref, gs_ref = refs[1 + 2 * n_seg:3 + 2 * n_seg]
    o_ref, s_scr = refs[-2:]
    bq = qt_ref.shape[1]
    lv = lam_ref[0]
    lam = (jnp.exp(jnp.sum(lv[0:1] * lv[1:2], axis=-1, keepdims=True))
           - jnp.exp(jnp.sum(lv[2:3] * lv[3:4], axis=-1, keepdims=True)) + lam_init)
    gs = gs_ref[0]
    lo = lax.broadcasted_iota(jnp.int32, (LANES, bq), 0) < DIFF_DIM
    blk = lambda h: slice(h * LANES, (h + 1) * LANES)

    def get_qt(h):
        qt = qt_ref[blk(h), :]
        zero = jnp.zeros_like(qt)
        return jnp.concatenate([jnp.where(lo, qt, zero), jnp.where(lo, zero, qt)], axis=1)

    def finish(h, oc):
        o = (oc[:, :bq] - lam * oc[:, bq:]).T
        o_ref[:, blk(h)] = (_rms(o, gs, DIFF_V) * (1.0 - lam_init)).astype(BF16)

    _flash_heads(DIFF_HEADS, get_qt, blk, blk, k_refs, vt_refs, s_scr, finish)


def _attn_call(kern, qt, k, vt, extra, *, width_qk, width_v, q_cols, batch, seq, ctx, r_lat, bq,
               latent, name):
    extra_specs = [pl.BlockSpec((1,) + a.shape[1:], functools.partial(lambda l, b, i: (l, 0, 0), lyr))
                   for a, lyr in extra]
    extra_args = [a for a, _ in extra]
    ctx_blk0 = r_lat // ctx
    if latent:
        nq = seq // bq
        grid = (batch, nq)
        q_spec = pl.BlockSpec((width_qk, bq), lambda b, i: (0, b * nq + i))
        k_specs = [pl.BlockSpec((ctx, width_qk), lambda b, i: (ctx_blk0 + b, 0)),
                   pl.BlockSpec((seq, width_qk), lambda b, i: (b, 0))]
        v_specs = [pl.BlockSpec((width_v, ctx), lambda b, i: (0, ctx_blk0 + b)),
                   pl.BlockSpec((width_v, seq), lambda b, i: (0, b))]
        o_spec = pl.BlockSpec((bq, width_v), lambda b, i: (b * nq + i, 0))
        k_args, v_args, o_rows = [k, k], [vt, vt], r_lat
    else:
        grid = (batch, 1)
        q_spec = pl.BlockSpec((width_qk, ctx), lambda b, i: (0, ctx_blk0 + b))
        k_specs = [pl.BlockSpec((ctx, width_qk), lambda b, i: (ctx_blk0 + b, 0))]
        v_specs = [pl.BlockSpec((width_v, ctx), lambda b, i: (0, ctx_blk0 + b))]
        o_spec = pl.BlockSpec((ctx, width_v), lambda b, i: (b, 0))
        k_args, v_args, o_rows = [k], [vt], batch * ctx
    return pl.pallas_call(
        functools.partial(kern, n_seg=len(k_args)),
        grid=grid,
        in_specs=[q_spec] + k_specs + v_specs + extra_specs,
        out_specs=o_spec,
        out_shape=jax.ShapeDtypeStruct((o_rows, width_v), BF16),
        scratch_shapes=[pltpu.VMEM((LOGIT_SLOTS, KEY_CHUNK, q_cols * q_spec.block_shape[1]), F32)],
        compiler_params=_params(("arbitrary", "arbitrary")),
        name=name,
    )(qt, *k_args, *v_args, *extra_args)


def _merge_kernel(x_ref, mod_ref, ng_ref, wgc_ref, oml_ref, omc_ref, odl_ref, odc_ref, u_ref, up_ref, un_ref,
                  cw_ref, wbr_ref, wo_ref, o_ref, *, n_lat_tiles, seq, ctx):
    i = pl.program_id(0)
    is_lat = i < n_lat_tiles
    tm, d = x_ref.shape
    x = x_ref[...]
    md = mod_ref[0, 0]
    h = _modulate(x, ng_ref[0, 1:2], md[3:4], md[4:5]).astype(BF16)

    u = u_ref[...]
    row = lax.broadcasted_iota(jnp.int32, (tm, 1), 0)
    seq_len = jnp.where(is_lat, seq, ctx)
    pos = (i * tm + row) & (seq_len - 1)
    u_dn = jnp.where(row == 0, up_ref[SUBLANES - 1:SUBLANES, :], pltpu.roll(u, 1, 0))
    u_dn = jnp.where(pos == 0, 0.0, u_dn)
    u_up = jnp.where(row == tm - 1, un_ref[0:1, :], pltpu.roll(u, tm - 1, 0))
    u_up = jnp.where(pos == seq_len - 1, 0.0, u_up)
    cw = cw_ref[0]
    y = cw[0:1] * u_dn + cw[1:2] * u + cw[2:3] * u_up
    conv = (_dot(h, wgc_ref[0, :, 3 * d:]) * y).astype(BF16)

    branches = (jnp.where(is_lat, oml_ref[...], omc_ref[...]), jnp.where(is_lat, odl_ref[...], odc_ref[...]), conv)
    m = None
    for n, yb in enumerate(branches):
        g = _sigmoid(_dot(h, wgc_ref[0, :, n * d:(n + 1) * d]))
        t = g * _dot(yb, wbr_ref[0, n])
        m = t if m is None else m + t
    o_ref[...] = x + md[5:6] * _dot(m.astype(BF16), wo_ref[0])


def _merge_call(xs, mod4, norm_g, wgc, om_lat, om_ctx, od_lat, od_ctx, u, conv_w, wbr, wo, *, layer, n_tiles,
                tm, grp, n_lat_tiles, seq, ctx):
    r, d = xs.shape
    hb = tm // SUBLANES
    last8 = u.shape[0] // SUBLANES - 1

    def rows(n):
        return pl.BlockSpec((tm, n), lambda i: (i, 0))

    lat_rows = pl.BlockSpec((tm, BRANCH_WIDTH), lambda i: (jnp.minimum(i, n_lat_tiles - 1), 0))
    ctx_rows = pl.BlockSpec((tm, BRANCH_WIDTH), lambda i: (jnp.maximum(i - n_lat_tiles, 0), 0))

    kern = functools.partial(_merge_kernel, n_lat_tiles=n_lat_tiles, seq=seq, ctx=ctx)
    return pl.pallas_call(
        kern,
        grid=(n_tiles,),
        in_specs=[
            rows(d),
            pl.BlockSpec((1, 1, N_MOD, d), lambda i: (layer, grp(i), 0, 0)),
            pl.BlockSpec((1, 3, d), lambda i: (layer, 0, 0)),
            _resident((1,) + wgc.shape[1:], lambda i: (layer, 0, 0)),
            lat_rows, ctx_rows, lat_rows, ctx_rows, rows(BRANCH_WIDTH),
            pl.BlockSpec((SUBLANES, BRANCH_WIDTH), lambda i: (jnp.maximum(i * hb - 1, 0), 0)),
            pl.BlockSpec((SUBLANES, BRANCH_WIDTH), lambda i: (jnp.minimum((i + 1) * hb, last8), 0)),
            pl.BlockSpec((1,) + conv_w.shape[1:], lambda i: (layer, 0, 0)),
            _resident((1,) + wbr.shape[1:], lambda i: (layer, 0, 0, 0)),
            _resident((1, d, d), lambda i: (layer, 0, 0)),
        ],
        out_specs=rows(d),
        out_shape=jax.ShapeDtypeStruct((n_tiles * tm, d), F32),
        compiler_params=_params(("arbitrary",)),
        name="merge",
    )(xs, mod4, norm_g, wgc, om_lat, om_ctx, od_lat, od_ctx, u, u, u, conv_w, wbr, wo)


def _rope_tables(seq, tm):
    t = jnp.arange(seq)
    row = (t // GRID_W).astype(F32)[:, None]
    col = (t % GRID_W).astype(F32)[:, None]

    lane = jnp.arange(LANES)

    def axis_tables(rot_dim, rel, live):
        q4 = rot_dim // 4
        inv = ROPE_THETA ** (-jnp.arange(q4, dtype=F32) / q4)
        ang = jnp.where((rel // (2 * q4)) == 0, row, col) * inv[rel % q4][None, :]
        first = ((rel // q4) % 2) == 0
        c = jnp.where(live, jnp.cos(ang), 1.0)
        s = jnp.where(live, jnp.where(first, -jnp.sin(ang), jnp.sin(ang)), 0.0)
        ident = [jnp.ones((tm, LANES), F32), jnp.zeros((tm, LANES), F32)]
        return [jnp.concatenate([a, b], axis=0) for a, b in zip((c, s), ident)]

    mla_live = (lane >= MLA_NOPE) & (lane < MLA_QK)
    mla = axis_tables(MLA_ROPE, jnp.where(mla_live, lane - MLA_NOPE, 0), mla_live)
    diff = axis_tables(DIFF_DIM, lane % DIFF_DIM, lane >= 0)
    return mla + diff


def _pad_last(a, n):
    return jnp.pad(a, [(0, 0)] * (a.ndim - 1) + [(0, n - a.shape[-1])])


def _tiles(batch, seq, n_ctx):
    tm, tm_ffn, bq_mla, bq_diff = 512, 1024, 512, 256
    r_ctx = batch * n_ctx
    assert all(seq % t == 0 and r_ctx % t == 0 for t in (tm, tm_ffn))
    assert seq % bq_mla == 0 and seq % bq_diff == 0 and (batch * seq) % n_ctx == 0
    assert n_ctx % min(KEY_CHUNK, n_ctx) == 0 and seq % KEY_CHUNK == 0
    assert seq & (seq - 1) == 0 and n_ctx & (n_ctx - 1) == 0 and seq % GRID_W == 0
    return tm, tm_ffn, bq_mla, bq_diff


def kernel(x, c, ctx, c_ctx, w_mod, b_mod, norm_g, ffn1_up, ffn1_down, ffn2_up, ffn2_down,
           w_in, g_cq, w_uq, g_ckv, w_ukv, g_q_mla, g_k_mla, g_q_diff, g_k_diff, lam,
           g_subln, conv_w, w_br, w_o):
    batch, seq, d = x.shape
    n_ctx = ctx.shape[1]
    depth = w_mod.shape[0]
    r_lat, r_ctx = batch * seq, batch * n_ctx
    tm, tm_ffn, bq_mla, bq_diff = _tiles(batch, seq, n_ctx)
    assert d == 1024 and w_br.shape[2] == BRANCH_WIDTH
    n_lat_tiles, n_all_tiles = r_lat // tm, (r_lat + r_ctx) // tm
    tiles_per_batch = seq // tm

    def grp_of(tile):
        return lambda i: jnp.where(i < r_lat // tile, i // (seq // tile), batch)

    grp, grp_ffn = grp_of(tm), grp_of(tm_ffn)

    def tab_blk(i):
        return jnp.where(i < n_lat_tiles, i % tiles_per_batch, tiles_per_batch)

    rank_q, rank_kv = g_cq.shape[1], g_ckv.shape[1]
    assert (rank_q, rank_kv) == (_T_CQ[1] - _T_CQ[0], _T_CKV[1] - _T_CKV[0])
    o = [0]
    for n in (rank_q, rank_kv, MLA_ROPE, 512, 512, 512, 512, 512, 512, 3 * d):
        o.append(o[-1] + n)
    seg = lambda k: w_in[:, :, o[k]:o[k + 1]]
    swap = lambda a: jnp.swapaxes(a, 1, 2)
    lanes = lambda g: jnp.broadcast_to(g[:, :, None], g.shape + (LANES,))
    wt = jnp.concatenate([seg(0), seg(1), seg(3), seg(5)], axis=-1).astype(BF16)

    def partner(a, rot_dim):
        q4 = rot_dim // 4
        g = a.reshape(a.shape[:-1] + (a.shape[-1] // rot_dim, 2, 2, q4))
        return jnp.flip(g, axis=-2).reshape(a.shape)

    def place_rope(a):
        return jnp.concatenate([jnp.zeros(a.shape[:-1] + (MLA_NOPE,), a.dtype), a,
                                jnp.zeros(a.shape[:-1] + (LANES - MLA_QK,), a.dtype)], axis=-1)

    wn = jnp.concatenate([seg(1), place_rope(seg(2)), seg(4), seg(7), seg(8)], axis=-1).astype(BF16)
    wgc = jnp.concatenate([seg(9), seg(6)], axis=-1).astype(BF16)
    wuq = _pad_last(w_uq.reshape(depth, rank_q, MLA_HEADS, MLA_QK), LANES)
    wuqt = swap(wuq.reshape(depth, rank_q, MLA_HEADS * LANES)).astype(BF16)
    wkv4 = w_ukv.reshape(depth, rank_kv, MLA_HEADS, MLA_NOPE + MLA_V)
    wk = _pad_last(wkv4[..., :MLA_NOPE], LANES).reshape(depth, rank_kv, MLA_HEADS * LANES).astype(BF16)
    wvt = swap(wkv4[..., MLA_NOPE:].reshape(depth, rank_kv, MLA_HEADS * MLA_V)).astype(BF16)
    ffn_w = [(ffn1_up.astype(BF16), ffn1_down.astype(BF16)), (ffn2_up.astype(BF16), ffn2_down.astype(BF16))]
    wbr = w_br.astype(BF16)
    wo = w_o.astype(BF16)
    gcqt = lanes(g_cq)
    gckvt = lanes(g_ckv)
    gckv = g_ckv[:, None, :]
    gqmt = lanes(_pad_last(g_q_mla, LANES))
    gkm = jnp.stack([_pad_last(g_k_mla, LANES),
                     place_rope(partner(g_k_mla[:, MLA_NOPE:], MLA_ROPE))], axis=1)
    gqdt = lanes(jnp.tile(g_q_diff, (1, 2)))
    gkd = jnp.stack([jnp.tile(g_k_diff, (1, 2)), jnp.tile(partner(g_k_diff, DIFF_DIM), (1, 2))], axis=1)
    gsub = g_subln[:, None, :]
    tabs = _rope_tables(seq, tm)
    tabs_t = [t.T for t in tabs]

    cond = jnp.concatenate([c, c_ctx[None, :]], axis=0)
    rows_pad = (batch + 1 + SUBLANES - 1) // SUBLANES * SUBLANES
    cond = jnp.pad(cond, ((0, rows_pad - batch - 1), (0, 0)))
    mod4 = _mod_call(cond, w_mod, b_mod).reshape(depth, rows_pad, N_MOD, d)

    xs = (x.reshape(r_lat, d), ctx.reshape(r_ctx, d))
    attn_kw = dict(batch=batch, seq=seq, ctx=n_ctx, r_lat=r_lat)
    for l in range(depth):
        last = l == depth - 1
        lam_init = 0.8 - 0.6 * math.exp(-0.3 * l)
        tm1, grp1 = (tm, grp) if l == 0 else (tm_ffn, grp_ffn)
        xs = _ffn_call(xs, mod4, norm_g, *ffn_w[0], layer=l, k0=0, gi=0,
                       n_tiles=n_all_tiles * tm // tm1, tm=tm1, grp=grp1)
        qmt, km, vmt, dqt, dk, dvt, u = _inproj_call(
            xs, mod4, norm_g, wt, wn, gcqt, wuqt, gckvt, wvt, gckv, wk, gqmt, gkm, gqdt, gkd, tabs, tabs_t,
            layer=l, n_tiles=n_all_tiles, tm=tm, grp=grp, tab_blk=tab_blk)
        diff_kern = functools.partial(_diff_attn_kernel, lam_init=lam_init)
        diff_extra = [(lam, l), (gsub, l)]
        mla_kw = dict(width_qk=MLA_HEADS * LANES, width_v=BRANCH_WIDTH, q_cols=1, bq=bq_mla, **attn_kw)
        diff_kw = dict(width_qk=DIFF_HEADS * LANES, width_v=BRANCH_WIDTH, q_cols=2, bq=bq_diff, **attn_kw)
        om = _attn_call(_mla_attn_kernel, qmt, km, vmt, [], latent=True, name="mla_lat", **mla_kw)
        od = _attn_call(diff_kern, dqt, dk, dvt, diff_extra, latent=True, name="diff_lat", **diff_kw)
        n_tiles, om_ctx, od_ctx = n_lat_tiles, om, od
        if not last:
            om_ctx = _attn_call(_mla_attn_kernel, qmt, km, vmt, [], latent=False, name="mla_ctx", **mla_kw)
            od_ctx = _attn_call(diff_kern, dqt, dk, dvt, diff_extra, latent=False, name="diff_ctx", **diff_kw)
            n_tiles = n_all_tiles
        xs = _merge_call(xs, mod4, norm_g, wgc, om, om_ctx, od, od_ctx, u, conv_w, wbr, wo, layer=l,
                         n_tiles=n_tiles, tm=tm, grp=grp, n_lat_tiles=n_lat_tiles, seq=seq, ctx=n_ctx)
        xs = _ffn_call(xs, mod4, norm_g, *ffn_w[1], layer=l, k0=6, gi=2,
                       n_tiles=n_tiles * tm // tm_ffn, tm=tm_ffn, grp=grp_ffn)
    return xs.reshape(batch, seq, d)
```

```python
import functools
import math

import jax
import jax.numpy as jnp
from jax import lax
from jax.experimental import pallas as pl
from jax.experimental.pallas import tpu as pltpu

GRID_W = 64
N_MOD = 9
MLA_HEADS = 8
MLA_NOPE = 64
MLA_ROPE = 32
MLA_QK = MLA_NOPE + MLA_ROPE
MLA_V = 64
DIFF_HEADS = 4
DIFF_DIM = 64
DIFF_V = 2 * DIFF_DIM
BRANCH_WIDTH = 512
ROPE_THETA = 10000.0
EPS = 1e-6
LOG2E = math.log2(math.e)

LANES = 128
SUBLANES = 8
BF16_ROWS = 16
MXU_COLS = 256
FFN_CHUNK_ELEMS = 768 * 1024
FFN_SPLIT = 2
VMEM_LIMIT_BYTES = 56 * 1024 * 1024
BF16 = jnp.bfloat16
F32 = jnp.float32


def _dot(a, b):
    return jnp.dot(a, b, preferred_element_type=F32)


def _rms(x, g, n):
    ss = jnp.sum(x * x, axis=-1, keepdims=True)
    return x * lax.rsqrt(ss * (1.0 / n) + EPS) * g


def _modulate(x, g, shift, scale):
    return _rms(x, g, x.shape[-1]) * (1.0 + scale) + shift


def _sigmoid(x):
    return 1.0 / (1.0 + jnp.exp(-x))


def _params(sem):
    return pltpu.CompilerParams(dimension_semantics=sem, vmem_limit_bytes=VMEM_LIMIT_BYTES)


def _resident(block_shape, index_map):
    return pl.BlockSpec(block_shape, index_map, pipeline_mode=pl.Buffered(1))


def _mod_kernel(cond_ref, w_ref, b_ref, o_ref):
    c = cond_ref[...]
    a = (c * _sigmoid(c)).astype(BF16)
    o_ref[0] = _dot(a, w_ref[0].astype(BF16)) + b_ref[0]


def _mod_call(cond, w_mod, b_mod):
    depth, d, nd = w_mod.shape
    rows = cond.shape[0]
    tn = d
    return pl.pallas_call(
        _mod_kernel,
        grid=(depth, nd // tn),
        in_specs=[
            pl.BlockSpec((rows, d), lambda l, j: (0, 0)),
            pl.BlockSpec((1, d, tn), lambda l, j: (l, 0, j)),
            pl.BlockSpec((1, 1, tn), lambda l, j: (l, 0, j)),
        ],
        out_specs=pl.BlockSpec((1, rows, tn), lambda l, j: (l, 0, j)),
        out_shape=jax.ShapeDtypeStruct((depth, rows, nd), F32),
        compiler_params=_params(("arbitrary", "arbitrary")),
        name="mod",
    )(cond, w_mod, b_mod.reshape(depth, 1, nd))


def _ffn_kernel(*refs, k0, gi, dff, chunks, n_first):
    x_refs, (mod_ref, ng_ref, wup_ref, wdn_ref, o_ref, u_scr) = refs[:-6], refs[-6:]
    md = mod_ref[0, 0]
    tn = o_ref.shape[0] // FFN_SPLIT
    for t0 in range(0, o_ref.shape[0], tn):
        ts = slice(t0, t0 + tn)
        x = x_refs[0][ts, :]
        if n_first is not None:
            x = jnp.where(pl.program_id(0) < n_first, x, x_refs[1][ts, :])
        h = _modulate(x, ng_ref[0, gi:gi + 1], md[k0:k0 + 1], md[k0 + 1:k0 + 2]).astype(BF16)
        for lo, hi in chunks:
            a = _dot(h, wup_ref[0, :, lo:hi])
            b = _dot(h, wup_ref[0, :, dff + lo:dff + hi])
            u_scr[ts, lo:hi] = (a * _sigmoid(a) * b).astype(BF16)
        y = _dot(u_scr[ts, :], wdn_ref[0])
        o_ref[ts, :] = x + (0.5 * md[k0 + 2:k0 + 3]) * y


def _ffn_call(xs, mod4, norm_g, wup, wdn, *, layer, k0, gi, n_tiles, tm, grp):
    xs = xs if isinstance(xs, tuple) else (xs,)
    d = xs[0].shape[1]
    dff = wdn.shape[1]
    width = max(MXU_COLS, FFN_CHUNK_ELEMS // (tm // FFN_SPLIT) // MXU_COLS * MXU_COLS)
    chunks = tuple((lo, min(lo + width, dff)) for lo in range(0, dff, width))
    n_first = xs[0].shape[0] // tm if len(xs) == 2 else None
    kern = functools.partial(_ffn_kernel, k0=k0, gi=gi, dff=dff, chunks=chunks, n_first=n_first)
    if n_first is None:
        x_specs = [pl.BlockSpec((tm, d), lambda i: (i, 0))]
    else:
        x_specs = [pl.BlockSpec((tm, d), lambda i: (jnp.minimum(i, n_first - 1), 0)),
                   pl.BlockSpec((tm, d), lambda i: (jnp.maximum(i - n_first, 0), 0))]
    return pl.pallas_call(
        kern,
        grid=(n_tiles,),
        in_specs=x_specs + [
            pl.BlockSpec((1, 1, N_MOD, d), lambda i: (layer, grp(i), 0, 0)),
            pl.BlockSpec((1, 3, d), lambda i: (layer, 0, 0)),
            _resident((1, d, 2 * dff), lambda i: (layer, 0, 0)),
            _resident((1, dff, d), lambda i: (layer, 0, 0)),
        ],
        out_specs=pl.BlockSpec((tm, d), lambda i: (i, 0)),
        out_shape=jax.ShapeDtypeStruct((n_tiles * tm, d), F32),
        scratch_shapes=[pltpu.VMEM((tm, dff), BF16)],
        compiler_params=_params(("arbitrary",)),
        name="ffn",
    )(*xs, mod4, norm_g, wup, wdn)


_T_CQ = (0, 384)
_T_CKV = (384, 640)
_T_DQ = (640, 1152)
_T_DV = (1152, 1664)
_N_CKV = (0, 256)
_N_KR = (256, 384)
_N_DK = (384, 896)
_N_CC = (896, 1408)
_N_CX = (1408, 1920)


def _partner_lanes(x, q4):
    first = (lax.broadcasted_iota(jnp.int32, x.shape, 1) & q4) == 0
    return jnp.where(first, pltpu.roll(x, LANES - q4, 1), pltpu.roll(x, q4, 1))


INPROJ_SPLIT = 2


def _tile_lanes(g, n):
    return jnp.concatenate([g] * (n // LANES), axis=1)


def _rms_t(x, g, n):
    ss = jnp.sum(x * x, axis=0, keepdims=True)
    return x * lax.rsqrt(ss * (1.0 / n) + EPS) * g


def _swap_row_blocks(x, width):
    parts = []
    for r0 in range(0, x.shape[0], 2 * width):
        parts += [x[r0 + width:r0 + 2 * width], x[r0:r0 + width]]
    return jnp.concatenate(parts, axis=0)


def _inproj_kernel(x_ref, mod_ref, ng_ref, wt_ref, wn_ref, gcqt_ref, wuqt_ref, gckvt_ref, wvt_ref,
                   gckv_ref, wk_ref, gqmt_ref, gkm_ref, gqdt_ref, gkd_ref,
                   cm_ref, sgm_ref, cd_ref, sgd_ref, cmt_ref, sgmt_ref, cdt_ref, sgdt_ref,
                   qmt_ref, km_ref, vmt_ref, dqt_ref, dk_ref, dvt_ref, u_ref, wt_scr):
    @pl.when(pl.program_id(0) == 0)
    def _():
        wt_scr[...] = wt_ref[0].T

    md = mod_ref[0, 0]
    tn = x_ref.shape[0] // INPROJ_SPLIT
    for t0 in range(0, x_ref.shape[0], tn):
        ts = slice(t0, t0 + tn)
        hf = _modulate(x_ref[ts, :], ng_ref[0, 1:2], md[3:4], md[4:5])
        h = hf.astype(BF16)
        ht = hf.T.astype(BF16)

        pt = _dot(wt_scr[...], ht)
        pn = _dot(h, wn_ref[0])

        def rows(seg):
            return pt[seg[0]:seg[1]]

        def cols(seg):
            return pn[:, seg[0]:seg[1]]

        cqn_t = _rms_t(rows(_T_CQ), _tile_lanes(gcqt_ref[0], tn), _T_CQ[1] - _T_CQ[0]).astype(BF16)
        q_raw_t = _dot(wuqt_ref[0], cqn_t)
        ckvn_t = _rms_t(rows(_T_CKV), _tile_lanes(gckvt_ref[0], tn), _T_CKV[1] - _T_CKV[0]).astype(BF16)
        vmt_ref[:, ts] = _dot(wvt_ref[0], ckvn_t).astype(BF16)
        dvt_ref[:, ts] = rows(_T_DV).astype(BF16)

        cmt, sgmt = cmt_ref[:, ts], sgmt_ref[:, ts]
        gqmt = _tile_lanes(gqmt_ref[0], tn)
        q_scale = MLA_QK ** -0.5 * LOG2E
        for hd in range(MLA_HEADS):
            blk = slice(hd * LANES, (hd + 1) * LANES)
            qn = _rms_t(q_raw_t[blk], gqmt, MLA_QK)
            qr = qn * cmt + _swap_row_blocks(qn, MLA_ROPE // 4) * sgmt
            qmt_ref[blk, ts] = (qr * q_scale).astype(BF16)

        cdt, sgdt = cdt_ref[:, ts], sgdt_ref[:, ts]
        gqdt = _tile_lanes(gqdt_ref[0], tn)
        d_scale = DIFF_DIM ** -0.5 * LOG2E
        dq_t = rows(_T_DQ)
        for hd in range(DIFF_HEADS):
            halves = []
            for comp in range(2):
                r0 = hd * LANES + comp * DIFF_DIM
                xc = dq_t[r0:r0 + DIFF_DIM]
                ss = jnp.sum(xc * xc, axis=0, keepdims=True)
                halves.append(xc * lax.rsqrt(ss * (1.0 / DIFF_DIM) + EPS))
            xn = jnp.concatenate(halves, axis=0) * gqdt
            xr = xn * cdt + _swap_row_blocks(xn, DIFF_DIM // 4) * sgdt
            dqt_ref[hd * LANES:(hd + 1) * LANES, ts] = (xr * d_scale).astype(BF16)

        ckvn = _rms(cols(_N_CKV), gckv_ref[0], _N_CKV[1] - _N_CKV[0]).astype(BF16)
        k_raw = _dot(ckvn, wk_ref[0])
        krope = cols(_N_KR)
        gc_m = gkm_ref[0, 0:1] * cm_ref[ts, :]
        rot_m = _partner_lanes(krope, MLA_ROPE // 4) * (gkm_ref[0, 1:2] * sgm_ref[ts, :])
        for hd in range(MLA_HEADS):
            blk = slice(hd * LANES, (hd + 1) * LANES)
            kf = k_raw[:, blk] + krope
            r = lax.rsqrt(jnp.sum(kf * kf, axis=-1, keepdims=True) * (1.0 / MLA_QK) + EPS)
            km_ref[ts, blk] = ((kf * gc_m + rot_m) * r).astype(BF16)

        gc_d = gkd_ref[0, 0:1] * cd_ref[ts, :]
        gs_d = gkd_ref[0, 1:2] * sgd_ref[ts, :]
        lo = lax.broadcasted_iota(jnp.int32, (tn, LANES), 1) < DIFF_DIM
        dk = cols(_N_DK)
        for hd in range(DIFF_HEADS):
            blk = slice(hd * LANES, (hd + 1) * LANES)
            xh = dk[:, blk]
            sq = xh * xh
            s_lo = jnp.sum(jnp.where(lo, sq, 0.0), axis=-1, keepdims=True)
            s_hi = jnp.sum(jnp.where(lo, 0.0, sq), axis=-1, keepdims=True)
            r = lax.rsqrt(jnp.where(lo, s_lo, s_hi) * (1.0 / DIFF_DIM) + EPS)
            dk_ref[ts, blk] = ((xh * gc_d + _partner_lanes(xh, DIFF_DIM // 4) * gs_d) * r).astype(BF16)

        u_ref[ts, :] = cols(_N_CC) * cols(_N_CX)


def _inproj_call(xs, mod4, norm_g, wt, wn, gcqt, wuqt, gckvt, wvt, gckv, wk, gqmt, gkm, gqdt, gkd, tabs, tabs_t,
                 *, layer, n_tiles, tm, grp, tab_blk):
    r, d = xs.shape

    def vec(n):
        return pl.BlockSpec((1, 1, n), lambda i: (layer, 0, 0))

    def res(a):
        return _resident((1,) + a.shape[1:], lambda i: (layer, 0, 0))

    def tab():
        return pl.BlockSpec((tm, LANES), lambda i: (tab_blk(i), 0))

    def tab_t():
        return pl.BlockSpec((LANES, tm), lambda i: (0, tab_blk(i)))

    def rows(n):
        return pl.BlockSpec((tm, n), lambda i: (i, 0))

    hm, hd = MLA_HEADS * LANES, DIFF_HEADS * LANES
    out_shapes = [
        jax.ShapeDtypeStruct((hm, r), BF16),
        jax.ShapeDtypeStruct((r, hm), BF16),
        jax.ShapeDtypeStruct((BRANCH_WIDTH, r), BF16),
        jax.ShapeDtypeStruct((hd, r), BF16),
        jax.ShapeDtypeStruct((r, hd), BF16),
        jax.ShapeDtypeStruct((hd, r), BF16),
        jax.ShapeDtypeStruct((r, BRANCH_WIDTH), F32),
    ]
    return pl.pallas_call(
        _inproj_kernel,
        grid=(n_tiles,),
        in_specs=[
            rows(d),
            pl.BlockSpec((1, 1, N_MOD, d), lambda i: (layer, grp(i), 0, 0)),
            pl.BlockSpec((1, 3, d), lambda i: (layer, 0, 0)),
            res(wt), res(wn), res(gcqt), res(wuqt), res(gckvt), res(wvt),
            vec(gckv.shape[2]), res(wk), res(gqmt), res(gkm), res(gqdt), res(gkd),
            tab(), tab(), tab(), tab(),
            tab_t(), tab_t(), tab_t(), tab_t(),
        ],
        out_specs=[rows(s.shape[1]) if s.shape[0] == r else pl.BlockSpec((s.shape[0], tm), lambda i: (0, i))
                   for s in out_shapes],
        out_shape=out_shapes,
        scratch_shapes=[pltpu.VMEM((wt.shape[2], wt.shape[1]), BF16)],
        compiler_params=_params(("arbitrary",)),
        name="inproj",
    )(xs, mod4, norm_g, wt, wn, gcqt, wuqt, gckvt, wvt, gckv, wk, gqmt, gkm, gqdt, gkd, *tabs, *tabs_t)


KEY_CHUNK = 256
LOGIT_SLOTS = 3


def _col_stat(x, op):
    t, n = x.shape
    r = 64 if t % 64 == 0 else SUBLANES
    return op(op(x.reshape(t // r, r, n), axis=0), axis=0, keepdims=True)


def _flash_heads(n_heads, get_qt, kblk, vrows, k_refs, vt_refs, s_scr, finish):
    depth = s_scr.shape[0]
    chunks = [(k_ref, vt_ref, c0, min(KEY_CHUNK, k_ref.shape[0])) for k_ref, vt_ref in zip(k_refs, vt_refs)
              for c0 in range(0, k_ref.shape[0], min(KEY_CHUNK, k_ref.shape[0]))]
    steps = [(h, ci) for h in range(n_heads) for ci in range(len(chunks))]

    def logits(n):
        h, ci = steps[n]
        k_ref, _, c0, ck = chunks[ci]
        s_scr[n % depth, 0:ck] = _dot(k_ref[c0:c0 + ck, kblk(h)], get_qt(h))

    for n in range(min(depth - 1, len(steps))):
        logits(n)
    m = acc = None
    for n, (h, ci) in enumerate(steps):
        if n + depth - 1 < len(steps):
            logits(n + depth - 1)
        _, vt_ref, c0, ck = chunks[ci]
        s = s_scr[n % depth, 0:ck]
        cm = _col_stat(s, jnp.max)
        m_new = cm if ci == 0 else jnp.maximum(m, cm)
        p = jnp.exp2(s - m_new).astype(BF16)
        vt1 = jnp.concatenate([vt_ref[vrows(h), c0:c0 + ck], jnp.ones((BF16_ROWS, ck), BF16)], axis=0)
        pv = _dot(vt1, p)
        acc = pv if ci == 0 else jnp.exp2(m - m_new) * acc + pv
        m = m_new
        if ci == len(chunks) - 1:
            dv = acc.shape[0] - BF16_ROWS
            finish(h, acc[:dv] * (1.0 / acc[dv:dv + 1]))


def _mla_attn_kernel(*refs, n_seg):
    qt_ref = refs[0]
    k_refs = refs[1:1 + n_seg]
    vt_refs = refs[1 + n_seg:1 + 2 * n_seg]
    o_ref, s_scr = refs[-2:]
    held = []

    def finish(h, ot):
        held.append(ot)
        if h % 2 == 1:
            pair_t = jnp.concatenate(held, axis=0)
            o_ref[:, (h // 2) * LANES:(h // 2 + 1) * LANES] = pair_t.T.astype(BF16)
            held.clear()

    _flash_heads(MLA_HEADS, lambda h: qt_ref[h * LANES:(h + 1) * LANES, :],
                 lambda h: slice(h * LANES, (h + 1) * LANES), lambda h: slice(h * MLA_V, (h + 1) * MLA_V),
                 k_refs, vt_refs, s_scr, finish)


def _diff_attn_kernel(*refs, n_seg, lam_init):
    qt_ref = refs[0]
    k_refs = refs[1:1 + n_seg]
    vt_refs = refs[1 + n_seg:1 + 2 * n_seg]
    lam_ref, gs_ref = refs[1 + 2 * n_seg:3 + 2 * n_seg]
    o_ref, s_scr, qcat_scr = refs[-3:]
    bq = qt_ref.shape[1]
    lv = lam_ref[0]
    lam = (jnp.exp(jnp.sum(lv[0:1] * lv[1:2], axis=-1, keepdims=True))
           - jnp.exp(jnp.sum(lv[2:3] * lv[3:4], axis=-1, keepdims=True)) + lam_init)
    gs = gs_ref[0]
    lo = lax.broadcasted_iota(jnp.int32, (LANES, bq), 0) < DIFF_DIM
    blk = lambda h: slice(h * LANES, (h + 1) * LANES)

    for h in range(DIFF_HEADS):
        qt = qt_ref[blk(h), :]
        zero = jnp.zeros_like(qt)
        qcat_scr[h] = jnp.concatenate([jnp.where(lo, qt, zero), jnp.where(lo, zero, qt)], axis=1)

    def get_qt(h):
        return qcat_scr[h]

    def finish(h, oc):
        o = (oc[:, :bq] - lam * oc[:, bq:]).T
        o_ref[:, blk(h)] = (_rms(o, gs, DIFF_V) * (1.0 - lam_init)).astype(BF16)

    _flash_heads(DIFF_HEADS, get_qt, blk, blk, k_refs, vt_refs, s_scr, finish)


def _attn_call(kern, qt, k, vt, extra, *, width_qk, width_v, q_cols, batch, seq, ctx, r_lat, bq,
               latent, name):
    extra_specs = [pl.BlockSpec((1,) + a.shape[1:], functools.partial(lambda l, b, i: (l, 0, 0), lyr))
                   for a, lyr in extra]
    extra_args = [a for a, _ in extra]
    ctx_blk0 = r_lat // ctx
    if latent:
        nq = seq // bq
        grid = (batch, nq)
        q_spec = pl.BlockSpec((width_qk, bq), lambda b, i: (0, b * nq + i))
        k_specs = [pl.BlockSpec((ctx, width_qk), lambda b, i: (ctx_blk0 + b, 0)),
                   pl.BlockSpec((seq, width_qk), lambda b, i: (b, 0))]
        v_specs = [pl.BlockSpec((width_v, ctx), lambda b, i: (0, ctx_blk0 + b)),
                   pl.BlockSpec((width_v, seq), lambda b, i: (0, b))]
        o_spec = pl.BlockSpec((bq, width_v), lambda b, i: (b * nq + i, 0))
        k_args, v_args, o_rows = [k, k], [vt, vt], r_lat
    else:
        grid = (batch, 1)
        q_spec = pl.BlockSpec((width_qk, ctx), lambda b, i: (0, ctx_blk0 + b))
        k_specs = [pl.BlockSpec((ctx, width_qk), lambda b, i: (ctx_blk0 + b, 0))]
        v_specs = [pl.BlockSpec((width_v, ctx), lambda b, i: (0, ctx_blk0 + b))]
        o_spec = pl.BlockSpec((ctx, width_v), lambda b, i: (b, 0))
        k_args, v_args, o_rows = [k], [vt], batch * ctx
    return pl.pallas_call(
        functools.partial(kern, n_seg=len(k_args)),
        grid=grid,
        in_specs=[q_spec] + k_specs + v_specs + extra_specs,
        out_specs=o_spec,
        out_shape=jax.ShapeDtypeStruct((o_rows, width_v), BF16),
        scratch_shapes=[pltpu.VMEM((LOGIT_SLOTS, KEY_CHUNK, q_cols * q_spec.block_shape[1]), F32)] + (
            [pltpu.VMEM((width_qk // LANES, LANES, q_cols * q_spec.block_shape[1]), BF16)] if q_cols > 1 else []),
        compiler_params=_params(("arbitrary", "arbitrary")),
        name=name,
    )(qt, *k_args, *v_args, *extra_args)


def _merge_kernel(x_ref, mod_ref, ng_ref, wgc_ref, oml_ref, omc_ref, odl_ref, odc_ref, u_ref, up_ref, un_ref,
                  cw_ref, wbr_ref, wo_ref, o_ref, *, n_lat_tiles, seq, ctx):
    i = pl.program_id(0)
    is_lat = i < n_lat_tiles
    tm, d = x_ref.shape
    x = x_ref[...]
    md = mod_ref[0, 0]
    h = _modulate(x, ng_ref[0, 1:2], md[3:4], md[4:5]).astype(BF16)

    u = u_ref[...]
    row = lax.broadcasted_iota(jnp.int32, (tm, 1), 0)
    seq_len = jnp.where(is_lat, seq, ctx)
    pos = (i * tm + row) & (seq_len - 1)
    u_dn = jnp.where(row == 0, up_ref[SUBLANES - 1:SUBLANES, :], pltpu.roll(u, 1, 0))
    u_dn = jnp.where(pos == 0, 0.0, u_dn)
    u_up = jnp.where(row == tm - 1, un_ref[0:1, :], pltpu.roll(u, tm - 1, 0))
    u_up = jnp.where(pos == seq_len - 1, 0.0, u_up)
    cw = cw_ref[0]
    y = cw[0:1] * u_dn + cw[1:2] * u + cw[2:3] * u_up
    conv = (_dot(h, wgc_ref[0, :, 3 * d:]) * y).astype(BF16)

    branches = (jnp.where(is_lat, oml_ref[...], omc_ref[...]), jnp.where(is_lat, odl_ref[...], odc_ref[...]), conv)
    m = None
    for n, yb in enumerate(branches):
        g = _sigmoid(_dot(h, wgc_ref[0, :, n * d:(n + 1) * d]))
        t = g * _dot(yb, wbr_ref[0, n])
        m = t if m is None else m + t
    o_ref[...] = x + md[5:6] * _dot(m.astype(BF16), wo_ref[0])


def _merge_call(xs, mod4, norm_g, wgc, om_lat, om_ctx, od_lat, od_ctx, u, conv_w, wbr, wo, *, layer, n_tiles,
                tm, grp, n_lat_tiles, seq, ctx):
    r, d = xs.shape
    hb = tm // SUBLANES
    last8 = u.shape[0] // SUBLANES - 1

    def rows(n):
        return pl.BlockSpec((tm, n), lambda i: (i, 0))

    lat_rows = pl.BlockSpec((tm, BRANCH_WIDTH), lambda i: (jnp.minimum(i, n_lat_tiles - 1), 0))
    ctx_rows = pl.BlockSpec((tm, BRANCH_WIDTH), lambda i: (jnp.maximum(i - n_lat_tiles, 0), 0))

    kern = functools.partial(_merge_kernel, n_lat_tiles=n_lat_tiles, seq=seq, ctx=ctx)
    return pl.pallas_call(
        kern,
        grid=(n_tiles,),
        in_specs=[
            rows(d),
            pl.BlockSpec((1, 1, N_MOD, d), lambda i: (layer, grp(i), 0, 0)),
            pl.BlockSpec((1, 3, d), lambda i: (layer, 0, 0)),
            _resident((1,) + wgc.shape[1:], lambda i: (layer, 0, 0)),
            lat_rows, ctx_rows, lat_rows, ctx_rows, rows(BRANCH_WIDTH),
            pl.BlockSpec((SUBLANES, BRANCH_WIDTH), lambda i: (jnp.maximum(i * hb - 1, 0), 0)),
            pl.BlockSpec((SUBLANES, BRANCH_WIDTH), lambda i: (jnp.minimum((i + 1) * hb, last8), 0)),
            pl.BlockSpec((1,) + conv_w.shape[1:], lambda i: (layer, 0, 0)),
            _resident((1,) + wbr.shape[1:], lambda i: (layer, 0, 0, 0)),
            _resident((1, d, d), lambda i: (layer, 0, 0)),
        ],
        out_specs=rows(d),
        out_shape=jax.ShapeDtypeStruct((n_tiles * tm, d), F32),
        compiler_params=_params(("arbitrary",)),
        name="merge",
    )(xs, mod4, norm_g, wgc, om_lat, om_ctx, od_lat, od_ctx, u, u, u, conv_w, wbr, wo)


def _rope_tables(seq, tm):
    t = jnp.arange(seq)
    row = (t // GRID_W).astype(F32)[:, None]
    col = (t % GRID_W).astype(F32)[:, None]

    lane = jnp.arange(LANES)

    def axis_tables(rot_dim, rel, live):
        q4 = rot_dim // 4
        inv = ROPE_THETA ** (-jnp.arange(q4, dtype=F32) / q4)
        ang = jnp.where((rel // (2 * q4)) == 0, row, col) * inv[rel % q4][None, :]
        first = ((rel // q4) % 2) == 0
        c = jnp.where(live, jnp.cos(ang), 1.0)
        s = jnp.where(live, jnp.where(first, -jnp.sin(ang), jnp.sin(ang)), 0.0)
        ident = [jnp.ones((tm, LANES), F32), jnp.zeros((tm, LANES), F32)]
        return [jnp.concatenate([a, b], axis=0) for a, b in zip((c, s), ident)]

    mla_live = (lane >= MLA_NOPE) & (lane < MLA_QK)
    mla = axis_tables(MLA_ROPE, jnp.where(mla_live, lane - MLA_NOPE, 0), mla_live)
    diff = axis_tables(DIFF_DIM, lane % DIFF_DIM, lane >= 0)
    return mla + diff


def _pad_last(a, n):
    return jnp.pad(a, [(0, 0)] * (a.ndim - 1) + [(0, n - a.shape[-1])])


def _tiles(batch, seq, n_ctx):
    tm, tm_ffn, bq_mla, bq_diff = 512, 1024, 512, 256
    r_ctx = batch * n_ctx
    assert all(seq % t == 0 and r_ctx % t == 0 for t in (tm, tm_ffn))
    assert seq % bq_mla == 0 and seq % bq_diff == 0 and (batch * seq) % n_ctx == 0
    assert n_ctx % min(KEY_CHUNK, n_ctx) == 0 and seq % KEY_CHUNK == 0
    assert seq & (seq - 1) == 0 and n_ctx & (n_ctx - 1) == 0 and seq % GRID_W == 0
    return tm, tm_ffn, bq_mla, bq_diff


def kernel(x, c, ctx, c_ctx, w_mod, b_mod, norm_g, ffn1_up, ffn1_down, ffn2_up, ffn2_down,
           w_in, g_cq, w_uq, g_ckv, w_ukv, g_q_mla, g_k_mla, g_q_diff, g_k_diff, lam,
           g_subln, conv_w, w_br, w_o):
    batch, seq, d = x.shape
    n_ctx = ctx.shape[1]
    depth = w_mod.shape[0]
    r_lat, r_ctx = batch * seq, batch * n_ctx
    tm, tm_ffn, bq_mla, bq_diff = _tiles(batch, seq, n_ctx)
    assert d == 1024 and w_br.shape[2] == BRANCH_WIDTH
    n_lat_tiles, n_all_tiles = r_lat // tm, (r_lat + r_ctx) // tm
    tiles_per_batch = seq // tm

    def grp_of(tile):
        return lambda i: jnp.where(i < r_lat // tile, i // (seq // tile), batch)

    grp, grp_ffn = grp_of(tm), grp_of(tm_ffn)

    def tab_blk(i):
        return jnp.where(i < n_lat_tiles, i % tiles_per_batch, tiles_per_batch)

    rank_q, rank_kv = g_cq.shape[1], g_ckv.shape[1]
    assert (rank_q, rank_kv) == (_T_CQ[1] - _T_CQ[0], _T_CKV[1] - _T_CKV[0])
    o = [0]
    for n in (rank_q, rank_kv, MLA_ROPE, 512, 512, 512, 512, 512, 512, 3 * d):
        o.append(o[-1] + n)
    seg = lambda k: w_in[:, :, o[k]:o[k + 1]]
    swap = lambda a: jnp.swapaxes(a, 1, 2)
    lanes = lambda g: jnp.broadcast_to(g[:, :, None], g.shape + (LANES,))
    wt = jnp.concatenate([seg(0), seg(1), seg(3), seg(5)], axis=-1).astype(BF16)

    def partner(a, rot_dim):
        q4 = rot_dim // 4
        g = a.reshape(a.shape[:-1] + (a.shape[-1] // rot_dim, 2, 2, q4))
        return jnp.flip(g, axis=-2).reshape(a.shape)

    def place_rope(a):
        return jnp.concatenate([jnp.zeros(a.shape[:-1] + (MLA_NOPE,), a.dtype), a,
                                jnp.zeros(a.shape[:-1] + (LANES - MLA_QK,), a.dtype)], axis=-1)

    wn = jnp.concatenate([seg(1), place_rope(seg(2)), seg(4), seg(7), seg(8)], axis=-1).astype(BF16)
    wgc = jnp.concatenate([seg(9), seg(6)], axis=-1).astype(BF16)
    wuq = _pad_last(w_uq.reshape(depth, rank_q, MLA_HEADS, MLA_QK), LANES)
    wuqt = swap(wuq.reshape(depth, rank_q, MLA_HEADS * LANES)).astype(BF16)
    wkv4 = w_ukv.reshape(depth, rank_kv, MLA_HEADS, MLA_NOPE + MLA_V)
    wk = _pad_last(wkv4[..., :MLA_NOPE], LANES).reshape(depth, rank_kv, MLA_HEADS * LANES).astype(BF16)
    wvt = swap(wkv4[..., MLA_NOPE:].reshape(depth, rank_kv, MLA_HEADS * MLA_V)).astype(BF16)
    ffn_w = [(ffn1_up.astype(BF16), ffn1_down.astype(BF16)), (ffn2_up.astype(BF16), ffn2_down.astype(BF16))]
    wbr = w_br.astype(BF16)
    wo = w_o.astype(BF16)
    gcqt = lanes(g_cq)
    gckvt = lanes(g_ckv)
    gckv = g_ckv[:, None, :]
    gqmt = lanes(_pad_last(g_q_mla, LANES))
    gkm = jnp.stack([_pad_last(g_k_mla, LANES),
                     place_rope(partner(g_k_mla[:, MLA_NOPE:], MLA_ROPE))], axis=1)
    gqdt = lanes(jnp.tile(g_q_diff, (1, 2)))
    gkd = jnp.stack([jnp.tile(g_k_diff, (1, 2)), jnp.tile(partner(g_k_diff, DIFF_DIM), (1, 2))], axis=1)
    gsub = g_subln[:, None, :]
    tabs = _rope_tables(seq, tm)
    tabs_t = [t.T for t in tabs]

    cond = jnp.concatenate([c, c_ctx[None, :]], axis=0)
    rows_pad = (batch + 1 + SUBLANES - 1) // SUBLANES * SUBLANES
    cond = jnp.pad(cond, ((0, rows_pad - batch - 1), (0, 0)))
    mod4 = _mod_call(cond, w_mod, b_mod).reshape(depth, rows_pad, N_MOD, d)

    xs = (x.reshape(r_lat, d), ctx.reshape(r_ctx, d))
    attn_kw = dict(batch=batch, seq=seq, ctx=n_ctx, r_lat=r_lat)
    for l in range(depth):
        last = l == depth - 1
        lam_init = 0.8 - 0.6 * math.exp(-0.3 * l)
        tm1, grp1 = (tm, grp) if l == 0 else (tm_ffn, grp_ffn)
        xs = _ffn_call(xs, mod4, norm_g, *ffn_w[0], layer=l, k0=0, gi=0,
                       n_tiles=n_all_tiles * tm // tm1, tm=tm1, grp=grp1)
        qmt, km, vmt, dqt, dk, dvt, u = _inproj_call(
            xs, mod4, norm_g, wt, wn, gcqt, wuqt, gckvt, wvt, gckv, wk, gqmt, gkm, gqdt, gkd, tabs, tabs_t,
            layer=l, n_tiles=n_all_tiles, tm=tm, grp=grp, tab_blk=tab_blk)
        diff_kern = functools.partial(_diff_attn_kernel, lam_init=lam_init)
        diff_extra = [(lam, l), (gsub, l)]
        mla_kw = dict(width_qk=MLA_HEADS * LANES, width_v=BRANCH_WIDTH, q_cols=1, bq=bq_mla, **attn_kw)
        diff_kw = dict(width_qk=DIFF_HEADS * LANES, width_v=BRANCH_WIDTH, q_cols=2, bq=bq_diff, **attn_kw)
        om = _attn_call(_mla_attn_kernel, qmt, km, vmt, [], latent=True, name="mla_lat", **mla_kw)
        od = _attn_call(diff_kern, dqt, dk, dvt, diff_extra, latent=True, name="diff_lat", **diff_kw)
        n_tiles, om_ctx, od_ctx = n_lat_tiles, om, od
        if not last:
            om_ctx = _attn_call(_mla_attn_kernel, qmt, km, vmt, [], latent=False, name="mla_ctx", **mla_kw)
            od_ctx = _attn_call(diff_kern, dqt, dk, dvt, diff_extra, latent=False, name="diff_ctx", **diff_kw)
            n_tiles = n_all_tiles
        xs = _merge_call(xs, mod4, norm_g, wgc, om, om_ctx, od, od_ctx, u, conv_w, wbr, wo, layer=l,
                         n_tiles=n_tiles, tm=tm, grp=grp, n_lat_tiles=n_lat_tiles, seq=seq, ctx=n_ctx)
        xs = _ffn_call(xs, mod4, norm_g, *ffn_w[1], layer=l, k0=6, gi=2,
                       n_tiles=n_tiles * tm // tm_ffn, tm=tm_ffn, grp=grp_ffn)
    return xs.reshape(batch, seq, d)
```
